```python
import jax, jax.numpy as jnp
from jax import lax
import numpy as np

D_MODEL = 1024
BATCH = 2
SEQ = 8192
DEPTH = 2
DEC_BATCH = 128
DEC_SEQ = 8
PAST_LEN = 8192
PAGE_SIZE = 128

N_HEADS = 16
QK_NOPE = 64
QK_ROPE = 32
V_HEAD = 64
Q_LORA = 512
KV_LORA = 256
ROPE_THETA = 10000.0
ATTN_SCALE = (QK_NOPE + QK_ROPE) ** -0.5
Q_BLOCK = 128
POOL_WINDOWS = (2, 4, 8, 16)
N_GROUPS = 4
GROUP_W = D_MODEL // N_GROUPS
POOL_BUF = 16 - 1
D_FF = 2816
CONV_W = 3
CONV_BUF = CONV_W - 1
N_MIXERS = 2
N_MLA_LAYERS = (DEPTH + 1) // 2
N_POOL_LAYERS = DEPTH // 2
EPS = 1e-6

kernel_name = 'hybrid_mla_pool_convffn_step'


def rms_norm(x, g):
    xf = x.astype(jnp.float32)
    y = xf * lax.rsqrt(jnp.mean(xf * xf, axis=-1, keepdims=True) + EPS)
    return (y * g.astype(jnp.float32)).astype(x.dtype)


def modulate(x, g, shift, scale):
    return rms_norm(x, g) * (1 + scale[:, None, :]) + shift[:, None, :]


def adaln(c, w, b):
    m = jax.nn.silu(c) @ w + b
    return jnp.split(m, 6, axis=-1)


def rope(x, pos):
    half = QK_ROPE // 2
    inv = ROPE_THETA ** (-jnp.arange(half, dtype=jnp.float32) / half)
    ang = pos.astype(jnp.float32)[:, None] * inv[None, :]
    shape = (1, ang.shape[0]) + (1,) * (x.ndim - 3) + (half,)
    cos = jnp.cos(ang).reshape(shape)
    sin = jnp.sin(ang).reshape(shape)
    x1 = x[..., :half].astype(jnp.float32)
    x2 = x[..., half:].astype(jnp.float32)
    out = jnp.concatenate([x1 * cos - x2 * sin, x2 * cos + x1 * sin], axis=-1)
    return out.astype(x.dtype)


def mla_project(h, pos, w_dq, q_norm, w_uq, w_dkv, kv_norm, w_uk):
    B, T, _ = h.shape
    cq = rms_norm(h @ w_dq, q_norm)
    q = (cq @ w_uq).reshape(B, T, N_HEADS, QK_NOPE + QK_ROPE)
    q_rope = rope(q[..., QK_NOPE:], pos)
    q_lat = jnp.einsum('bthn,khn->bthk', q[..., :QK_NOPE], w_uk)
    kv = h @ w_dkv
    ckv = rms_norm(kv[..., :KV_LORA], kv_norm)
    kr = rope(kv[..., KV_LORA:], pos)
    return q_lat, q_rope, ckv, kr


def mla_prompt_attend(q_lat, q_rope, ckv, kr):
    B, T = q_lat.shape[0], q_lat.shape[1]
    key_pos = jnp.arange(T)

    def block(i):
        s0 = i * Q_BLOCK
        qb = lax.dynamic_slice_in_dim(q_lat, s0, Q_BLOCK, axis=1)
        qrb = lax.dynamic_slice_in_dim(q_rope, s0, Q_BLOCK, axis=1)
        s = (jnp.einsum('bqhk,bsk->bhqs', qb, ckv)
             + jnp.einsum('bqhr,bsr->bhqs', qrb, kr)).astype(jnp.float32) * ATTN_SCALE
        qpos = s0 + jnp.arange(Q_BLOCK)
        s = jnp.where(key_pos[None, :] <= qpos[:, None], s, -jnp.inf)
        p = jax.nn.softmax(s, axis=-1).astype(ckv.dtype)
        return jnp.einsum('bhqs,bsk->bqhk', p, ckv)

    o = lax.map(block, jnp.arange(T // Q_BLOCK))
    return o.transpose(1, 0, 2, 3, 4).reshape(B, T, N_HEADS, KV_LORA)


def mla_sample_attend(q_lat, q_rope, ckv_new, kr_new, ckv_past, kr_past):
    S = q_lat.shape[1]
    P = ckv_past.shape[1]
    s_past = (jnp.einsum('bqhk,bsk->bhqs', q_lat, ckv_past)
              + jnp.einsum('bqhr,bsr->bhqs', q_rope, kr_past)).astype(jnp.float32) * ATTN_SCALE
    s_new = (jnp.einsum('bqhk,bsk->bhqs', q_lat, ckv_new)
             + jnp.einsum('bqhr,bsr->bhqs', q_rope, kr_new)).astype(jnp.float32) * ATTN_SCALE
    causal = jnp.tril(jnp.ones((S, S), dtype=bool))
    s_new = jnp.where(causal, s_new, -jnp.inf)
    p = jax.nn.softmax(jnp.concatenate([s_past, s_new], axis=-1), axis=-1).astype(ckv_new.dtype)
    return (jnp.einsum('bhqs,bsk->bqhk', p[..., :P], ckv_past)
            + jnp.einsum('bhqs,bsk->bqhk', p[..., P:], ckv_new))


def mla_out(o_lat, w_uv, w_o):
    B, T = o_lat.shape[0], o_lat.shape[1]
    o = jnp.einsum('bthk,khv->bthv', o_lat, w_uv).reshape(B, T, N_HEADS * V_HEAD)
    return o @ w_o


def pool_mix(h, prev, pos0, w_grp, ls):
    B, T, D = h.shape
    xp = jnp.concatenate([prev.astype(h.dtype), h], axis=1)
    xf = xp.astype(jnp.float32)
    cs = jnp.concatenate([jnp.zeros((B, 1, D), jnp.float32), jnp.cumsum(xf, axis=1)], axis=1)
    pos = pos0 + jnp.arange(T)
    outs = []
    for g, w in enumerate(POOL_WINDOWS):
        sl = slice(g * GROUP_W, (g + 1) * GROUP_W)
        win_sum = cs[:, POOL_BUF + 1:, sl] - cs[:, POOL_BUF + 1 - w:POOL_BUF + 1 - w + T, sl]
        cnt = jnp.minimum(pos + 1, w).astype(jnp.float32)[None, :, None]
        pooled = (win_sum / cnt - xf[:, POOL_BUF:, sl]).astype(h.dtype)
        outs.append(jnp.einsum('btc,cd->btd', pooled, w_grp[g]))
    return jnp.concatenate(outs, axis=-1) * ls, xp[:, -POOL_BUF:]


def conv_ffn(h, prev, w_up, conv_w, conv_b, w_down):
    T = h.shape[1]
    u = h @ w_up
    up = jnp.concatenate([prev.astype(u.dtype), u], axis=1)
    conv = conv_b + up[:, 0:T] * conv_w[0]
    for k in range(1, CONV_W):
        conv = conv + up[:, k:k + T] * conv_w[k]
    y = (jax.nn.silu(conv[..., :D_FF]) * conv[..., D_FF:]) @ w_down
    return y, up[:, -CONV_BUF:]


def setup_inputs(seed: int = 0) -> dict:
    key = jax.random.key(seed)
    ks = jax.random.split(key, 40)
    f32 = jnp.float32
    n_pages = PAST_LEN // PAGE_SIZE
    used_pages = DEC_BATCH * n_pages
    n_pool_pages = used_pages + used_pages // 4
    nrm = lambda k, shape, s: jax.random.normal(k, shape, f32) * s
    page_table = jax.random.permutation(ks[0], n_pool_pages)[:used_pages].reshape(DEC_BATCH, n_pages).astype(jnp.int32)
    return {
        'x_prompt': nrm(ks[1], (BATCH, SEQ, D_MODEL), 1.0),
        'x_sample': nrm(ks[2], (DEC_BATCH, DEC_SEQ, D_MODEL), 1.0),
        'cache_kv_latent': nrm(ks[3], (N_MLA_LAYERS, n_pool_pages, PAGE_SIZE, KV_LORA), 1.0),
        'cache_k_rope': nrm(ks[4], (N_MLA_LAYERS, n_pool_pages, PAGE_SIZE, QK_ROPE), 1.0),
        'state_pool': nrm(ks[5], (N_POOL_LAYERS, DEC_BATCH, POOL_BUF, D_MODEL), 1.0),
        'state_conv': nrm(ks[6], (DEPTH, DEC_BATCH, CONV_BUF, 2 * D_FF), 1.0),
        'page_table': page_table,
        'c_prompt': nrm(ks[7], (BATCH, D_MODEL), 1.0),
        'c_sample': nrm(ks[8], (DEC_BATCH, D_MODEL), 1.0),
        'ada_w': nrm(ks[9], (DEPTH, D_MODEL, 6 * D_MODEL), 0.5 * D_MODEL ** -0.5),
        'ada_b': nrm(ks[10], (DEPTH, 6 * D_MODEL), 0.02),
        'norm_mix_pre': 1.0 + nrm(ks[11], (DEPTH, D_MODEL), 0.1),
        'norm_mix_post': 1.0 + nrm(ks[12], (DEPTH, D_MODEL), 0.1),
        'norm_ffn_pre': 1.0 + nrm(ks[13], (DEPTH, D_MODEL), 0.1),
        'norm_ffn_post': 1.0 + nrm(ks[14], (DEPTH, D_MODEL), 0.1),
        'mla_w_dq': nrm(ks[15], (N_MLA_LAYERS, D_MODEL, Q_LORA), D_MODEL ** -0.5),
        'mla_q_norm': 1.0 + nrm(ks[16], (N_MLA_LAYERS, Q_LORA), 0.1),
        'mla_w_uq': nrm(ks[17], (N_MLA_LAYERS, Q_LORA, N_HEADS * (QK_NOPE + QK_ROPE)), Q_LORA ** -0.5),
        'mla_w_dkv': nrm(ks[18], (N_MLA_LAYERS, D_MODEL, KV_LORA + QK_ROPE), D_MODEL ** -0.5),
        'mla_kv_norm': 1.0 + nrm(ks[19], (N_MLA_LAYERS, KV_LORA), 0.1),
        'mla_w_uk': nrm(ks[20], (N_MLA_LAYERS, KV_LORA, N_HEADS, QK_NOPE), KV_LORA ** -0.5),
        'mla_w_uv': nrm(ks[21], (N_MLA_LAYERS, KV_LORA, N_HEADS, V_HEAD), KV_LORA ** -0.5),
        'mla_w_o': nrm(ks[22], (N_MLA_LAYERS, N_HEADS * V_HEAD, D_MODEL), (N_HEADS * V_HEAD) ** -0.5),
        'pool_w': nrm(ks[23], (N_POOL_LAYERS, N_GROUPS, GROUP_W, GROUP_W), GROUP_W ** -0.5),
        'pool_scale': 1.0 + nrm(ks[24], (N_POOL_LAYERS, D_MODEL), 0.1),
        'ffn_w_up': nrm(ks[25], (DEPTH, D_MODEL, 2 * D_FF), D_MODEL ** -0.5),
        'ffn_conv_w': nrm(ks[26], (DEPTH, CONV_W, 2 * D_FF), CONV_W ** -0.5),
        'ffn_conv_b': nrm(ks[27], (DEPTH, 2 * D_FF), 0.02),
        'ffn_w_down': nrm(ks[28], (DEPTH, D_FF, D_MODEL), D_FF ** -0.5),
    }


def reference(x_prompt, x_sample, cache_kv_latent, cache_k_rope, state_pool, state_conv, page_table,
              c_prompt, c_sample, ada_w, ada_b, norm_mix_pre, norm_mix_post, norm_ffn_pre, norm_ffn_post,
              mla_w_dq, mla_q_norm, mla_w_uq, mla_w_dkv, mla_kv_norm, mla_w_uk, mla_w_uv, mla_w_o,
              pool_w, pool_scale, ffn_w_up, ffn_conv_w, ffn_conv_b, ffn_w_down):
    B, T = x_prompt.shape[0], x_prompt.shape[1]
    Bd, S = x_sample.shape[0], x_sample.shape[1]
    n_pages = page_table.shape[1]
    past = n_pages * PAGE_SIZE
    pos_p = jnp.arange(T)
    pos_s = past + jnp.arange(S)
    pool_zero = jnp.zeros((B, POOL_BUF, D_MODEL), x_prompt.dtype)
    conv_zero = jnp.zeros((B, CONV_BUF, 2 * D_FF), x_prompt.dtype)

    xp, xs = x_prompt, x_sample
    kv_p, kr_p, kv_s, kr_s = [], [], [], []
    pool_p, pool_s, conv_p, conv_s = [], [], [], []
    for i in range(DEPTH):
        sh_mp, sc_mp, g_mp, sh_fp, sc_fp, g_fp = adaln(c_prompt, ada_w[i], ada_b[i])
        sh_ms, sc_ms, g_ms, sh_fs, sc_fs, g_fs = adaln(c_sample, ada_w[i], ada_b[i])
        hp = modulate(xp, norm_mix_pre[i], sh_mp, sc_mp)
        hs = modulate(xs, norm_mix_pre[i], sh_ms, sc_ms)
        j = i // N_MIXERS
        if i % N_MIXERS == 0:
            ql, qr, ckv, kr = mla_project(hp, pos_p, mla_w_dq[j], mla_q_norm[j], mla_w_uq[j],
                                          mla_w_dkv[j], mla_kv_norm[j], mla_w_uk[j])
            op = mla_out(mla_prompt_attend(ql, qr, ckv, kr), mla_w_uv[j], mla_w_o[j])
            qls, qrs, ckvs, krs = mla_project(hs, pos_s, mla_w_dq[j], mla_q_norm[j], mla_w_uq[j],
                                              mla_w_dkv[j], mla_kv_norm[j], mla_w_uk[j])
            ckv_past = cache_kv_latent[j, page_table].reshape(Bd, past, KV_LORA).astype(hs.dtype)
            kr_past = cache_k_rope[j, page_table].reshape(Bd, past, QK_ROPE).astype(hs.dtype)
            os_ = mla_out(mla_sample_attend(qls, qrs, ckvs, krs, ckv_past, kr_past), mla_w_uv[j], mla_w_o[j])
            kv_p.append(ckv); kr_p.append(kr); kv_s.append(ckvs); kr_s.append(krs)
        else:
            op, st_p = pool_mix(hp, pool_zero, 0, pool_w[j], pool_scale[j])
            os_, st_s = pool_mix(hs, state_pool[j], past, pool_w[j], pool_scale[j])
            pool_p.append(st_p); pool_s.append(st_s)
        xp = xp + g_mp[:, None, :] * rms_norm(op, norm_mix_post[i])
        xs = xs + g_ms[:, None, :] * rms_norm(os_, norm_mix_post[i])

        hp = modulate(xp, norm_ffn_pre[i], sh_fp, sc_fp)
        hs = modulate(xs, norm_ffn_pre[i], sh_fs, sc_fs)
        fp, cst_p = conv_ffn(hp, conv_zero, ffn_w_up[i], ffn_conv_w[i], ffn_conv_b[i], ffn_w_down[i])
        fs, cst_s = conv_ffn(hs, state_conv[i], ffn_w_up[i], ffn_conv_w[i], ffn_conv_b[i], ffn_w_down[i])
        xp = xp + g_fp[:, None, :] * rms_norm(fp, norm_ffn_post[i])
        xs = xs + g_fs[:, None, :] * rms_norm(fs, norm_ffn_post[i])
        conv_p.append(cst_p); conv_s.append(cst_s)

    new_kv_p = jnp.stack(kv_p)
    new_kr_p = jnp.stack(kr_p)
    new_kv_s = jnp.stack(kv_s)
    new_kr_s = jnp.stack(kr_s)
    new_pool_p = jnp.stack(pool_p)
    new_pool_s = jnp.stack(pool_s)
    new_conv_p = jnp.stack(conv_p)
    new_conv_s = jnp.stack(conv_s)
    return (xp, xs, new_kv_p, new_kr_p, new_kv_s, new_kr_s, new_pool_p, new_pool_s, new_conv_p, new_conv_s)
```

```python
import functools

import jax
import jax.numpy as jnp
from jax import lax
from jax.experimental import pallas as pl
from jax.experimental.pallas import tpu as pltpu

F32 = jnp.float32
BF16 = jnp.bfloat16

D_MODEL = 1024
N_HEADS = 16
QK_NOPE = 64
QK_ROPE = 32
V_HEAD = 64
Q_LORA = 512
KV_LORA = 256
ROPE_THETA = 10000.0
ATTN_SCALE = (QK_NOPE + QK_ROPE) ** -0.5
PAGE_SIZE = 128
POOL_WINDOWS = (2, 4, 8, 16)
GROUP_W = D_MODEL // len(POOL_WINDOWS)
POOL_BUF = 15
D_FF = 2816
CONV_W = 3
CONV_BUF = CONV_W - 1
EPS = 1e-6

HEAD_PAD = 128
POOL_HALO = 16
CONV_HALO = 8
FF_CHUNK = 256
NEW_KEY_ROWS = 16
PROMPT_TILE = 512
SAMPLE_TILE = 64
FLASH_TILE = 256
VMEM_LIMIT = 56 * 1024 * 1024


def _cparams(sem):
    return pltpu.CompilerParams(dimension_semantics=sem, vmem_limit_bytes=VMEM_LIMIT)


def _dot(a, b):
    return jnp.dot(a, b, preferred_element_type=F32)


def _dot_t(a, b):
    return lax.dot_general(a, b, (((1,), (1,)), ((), ())), preferred_element_type=F32)


def _rms(x, g):
    return x * lax.rsqrt(jnp.mean(x * x, axis=-1, keepdims=True) + EPS) * g


def _modulate(x, g, shift, scale):
    return _rms(x, g) * (1.0 + scale) + shift


def _silu(x):
    return x * (1.0 / (1.0 + jnp.exp(-x)))


def _resident(shape):
    zeros = (0,) * len(shape)
    return pl.BlockSpec(shape, lambda *_: zeros, pipeline_mode=pl.Buffered(1))


def _ld(ref):
    v = ref[...]
    return v if v.ndim == 2 else v.reshape(-1, v.shape[-1])


def _st(ref, val):
    ref[...] = val.reshape(ref.shape).astype(ref.dtype)


def _mod(mod_ref, k, rows):
    m = mod_ref[k]
    if m.shape[0] in (1, rows):
        return m
    return jnp.concatenate([m] * (rows // m.shape[0]), axis=0)


class _Rows:
    def __init__(self, x, seq=None):
        self.sample = x.ndim == 3
        if self.sample:
            self.steps, self.nb = x.shape[0], x.shape[1]
            self.nbt = min(SAMPLE_TILE, self.nb)
            self.grid = self.nb // self.nbt
            self.tile_rows = self.steps * self.nbt
        else:
            self.seq = seq
            self.tm = min(PROMPT_TILE, seq)
            self.grid = x.shape[0] // self.tm
            self.tps = seq // self.tm
            self.tile_rows = self.tm

    def spec(self, w):
        if self.sample:
            return pl.BlockSpec((self.steps, self.nbt, w), lambda i: (0, i, 0))
        return pl.BlockSpec((self.tm, w), lambda i: (i, 0))

    def shape(self, x, w, dtype):
        return jax.ShapeDtypeStruct(x.shape[:-1] + (w,), dtype)

    def mod_spec(self):
        if self.sample:
            return pl.BlockSpec((3, self.nbt, D_MODEL), lambda i: (0, i, 0))
        tps = self.tps
        return pl.BlockSpec((3, None, 1, D_MODEL), lambda i: (0, i // tps, 0, 0))

    def table_spec(self, w):
        if self.sample:
            return _resident((self.tile_rows, w))
        tps = self.tps
        return pl.BlockSpec((self.tm, w), lambda i: (i % tps, 0))


def _adaln_kernel(c_ref, w_ref, b_ref, o_ref):
    a = _silu(c_ref[...]).astype(BF16)
    o_ref[...] = _dot(a, w_ref[...]) + b_ref[...]


def _adaln(c_all, w, b):
    depth, d, n = w.shape
    rows = c_all.shape[0]
    tn = 1536
    return pl.pallas_call(
        _adaln_kernel,
        grid=(depth, n // tn),
        in_specs=[
            pl.BlockSpec((rows, d), lambda i, j: (0, 0)),
            pl.BlockSpec((None, d, tn), lambda i, j: (i, 0, j)),
            pl.BlockSpec((None, 1, tn), lambda i, j: (i, 0, j)),
        ],
        out_specs=pl.BlockSpec((None, rows, tn), lambda i, j: (i, 0, j)),
        out_shape=jax.ShapeDtypeStruct((depth, rows, n), F32),
        compiler_params=_cparams(("parallel", "parallel")),
        name="adaln",
    )(c_all, w, b.reshape(depth, 1, n))


def _proj_common(x_ref, mod_ref, npre_ref, cosk, sink, wdq_ref, qn_ref, wkv_ref, kvn_ref, ckv_ref, kr_ref):
    x = _ld(x_ref)
    rows = x.shape[0]
    h = _modulate(x, npre_ref[...], _mod(mod_ref, 0, rows), _mod(mod_ref, 1, rows)).astype(BF16)
    cq = _rms(_dot(h, wdq_ref[...]), qn_ref[...]).astype(BF16)
    kv = _dot(h, wkv_ref[...])
    ckv = _rms(kv[:, :KV_LORA], kvn_ref[...])
    krp = kv[:, KV_LORA:KV_LORA + HEAD_PAD] * cosk + kv[:, KV_LORA + HEAD_PAD:] * sink
    _st(ckv_ref, ckv)
    _st(kr_ref, krp[:, :QK_ROPE])
    return cq, ckv, krp


def _proj_prompt_kernel(x_ref, mod_ref, npre_ref, cos_ref, sin_ref, wdq_ref, qn_ref, wqa_ref, wqb_ref,
                        wkv_ref, kvn_ref, wk_ref, wv_ref,
                        q_ref, k_ref, v_ref, ckv_ref, kr_ref):
    cos = cos_ref[...]
    sin = sin_ref[...]
    cq, ckv, krp = _proj_common(x_ref, mod_ref, npre_ref, cos, sin, wdq_ref, qn_ref, wkv_ref, kvn_ref,
                                ckv_ref, kr_ref)
    qa = _dot(cq, wqa_ref[...])
    qb = _dot(cq, wqb_ref[...])
    for hd in range(N_HEADS):
        sl = slice(hd * HEAD_PAD, (hd + 1) * HEAD_PAD)
        q_ref[:, sl] = (qa[:, sl] * cos + qb[:, sl] * sin).astype(q_ref.dtype)
    ckv_b = ckv.astype(BF16)
    kcat = jnp.concatenate([ckv_b, krp.astype(BF16)], axis=-1)
    k_ref[...] = _dot(kcat, wk_ref[...]).astype(k_ref.dtype)
    v_ref[...] = _dot(ckv_b, wv_ref[...]).astype(v_ref.dtype)


def _proj_prompt(x2d, mod, npre, cos_t, sin_t, wdq, qn, wqa, wqb, wkv, kvn, wk, wv, *, seq):
    r = _Rows(x2d, seq)
    hq = N_HEADS * HEAD_PAD
    return pl.pallas_call(
        _proj_prompt_kernel,
        grid=(r.grid,),
        in_specs=[
            r.spec(D_MODEL), r.mod_spec(), _resident((1, D_MODEL)),
            r.table_spec(HEAD_PAD), r.table_spec(HEAD_PAD),
            _resident(wdq.shape), _resident(qn.shape), _resident(wqa.shape), _resident(wqb.shape),
            _resident(wkv.shape), _resident(kvn.shape), _resident(wk.shape), _resident(wv.shape),
        ],
        out_specs=[r.spec(hq), r.spec(hq), r.spec(N_HEADS * V_HEAD), r.spec(KV_LORA), r.spec(QK_ROPE)],
        out_shape=[r.shape(x2d, hq, BF16), r.shape(x2d, hq, BF16), r.shape(x2d, N_HEADS * V_HEAD, BF16),
                   r.shape(x2d, KV_LORA, F32), r.shape(x2d, QK_ROPE, F32)],
        compiler_params=_cparams(("parallel",)),
        name="mla_proj_prompt",
    )(x2d, mod, npre, cos_t, sin_t, wdq, qn, wqa, wqb, wkv, kvn, wk, wv)


def _proj_sample_kernel(x_ref, mod_ref, npre_ref, cosq_ref, sinq_ref, cosk_ref, sink_ref,
                        wdq_ref, qn_ref, wqn_ref, wqa_ref, wqb_ref, wkv_ref, kvn_ref, bduk_ref,
                        ql_ref, qr_ref, ckv_ref, kr_ref):
    cq, _, _ = _proj_common(x_ref, mod_ref, npre_ref, cosk_ref[...], sink_ref[...], wdq_ref, qn_ref, wkv_ref,
                            kvn_ref, ckv_ref, kr_ref)
    q_nope = _dot(cq, wqn_ref[...]).astype(BF16)
    _st(ql_ref, _dot(q_nope, bduk_ref[...]))
    _st(qr_ref, _dot(cq, wqa_ref[...]) * cosq_ref[...] + _dot(cq, wqb_ref[...]) * sinq_ref[...])


def _proj_sample(x3d, mod, npre, cosq, sinq, cosk, sink, wdq, qn, wqn, wqa, wqb, wkv, kvn, bduk):
    r = _Rows(x3d)
    return pl.pallas_call(
        _proj_sample_kernel,
        grid=(r.grid,),
        in_specs=[
            r.spec(D_MODEL), r.mod_spec(), _resident((1, D_MODEL)),
            r.table_spec(N_HEADS * QK_ROPE), r.table_spec(N_HEADS * QK_ROPE),
            r.table_spec(HEAD_PAD), r.table_spec(HEAD_PAD),
            _resident(wdq.shape), _resident(qn.shape), _resident(wqn.shape), _resident(wqa.shape),
            _resident(wqb.shape), _resident(wkv.shape), _resident(kvn.shape), _resident(bduk.shape),
        ],
        out_specs=[r.spec(N_HEADS * KV_LORA), r.spec(N_HEADS * QK_ROPE), r.spec(KV_LORA), r.spec(QK_ROPE)],
        out_shape=[r.shape(x3d, N_HEADS * KV_LORA, BF16), r.shape(x3d, N_HEADS * QK_ROPE, BF16),
                   r.shape(x3d, KV_LORA, F32), r.shape(x3d, QK_ROPE, F32)],
        compiler_params=_cparams(("parallel",)),
        name="mla_proj_sample",
    )(x3d, mod, npre, cosq, sinq, cosk, sink, wdq, qn, wqn, wqa, wqb, wkv, kvn, bduk)


def _flash_kernel(q_ref, k_ref, v_ref, o_ref, *, tq):
    qi = pl.program_id(2)
    row = lax.broadcasted_iota(jnp.int32, (tq, tq), 0)
    col = lax.broadcasted_iota(jnp.int32, (tq, tq), 1)
    outs = []
    for hh in range(2):
        hs = slice(hh * HEAD_PAD, (hh + 1) * HEAD_PAD)
        q = q_ref[:, hs]

        def step(ki, carry, masked, hs=hs, q=q):
            m, l, acc = carry
            k0 = pl.multiple_of(ki * tq, tq)
            s = _dot_t(q, k_ref[pl.ds(k0, tq), hs])
            if masked:
                s = jnp.where(col <= row, s, -jnp.inf)
            m_new = jnp.maximum(m, jnp.max(s, axis=-1, keepdims=True))
            alpha = jnp.exp(m - m_new)
            p = jnp.exp(s - m_new)
            l = alpha * l + jnp.sum(p, axis=-1, keepdims=True)
            acc = alpha * acc + _dot(p.astype(BF16), v_ref[pl.ds(k0, tq), :])
            return m_new, l, acc

        init = (jnp.full((tq, 1), -jnp.inf, F32), jnp.zeros((tq, 1), F32), jnp.zeros((tq, 2 * V_HEAD), F32))
        carry = lax.fori_loop(0, qi, functools.partial(step, masked=False), init)
        _, l, acc = step(qi, carry, True)
        outs.append(acc / l)
    lane = lax.broadcasted_iota(jnp.int32, (tq, 2 * V_HEAD), 1)
    o_ref[...] = jnp.where(lane < V_HEAD, outs[0], outs[1]).astype(o_ref.dtype)


def _flash(q, k, v):
    nb, seq, _ = q.shape
    tq = min(FLASH_TILE, seq)
    return pl.pallas_call(
        functools.partial(_flash_kernel, tq=tq),
        grid=(nb, N_HEADS // 2, seq // tq),
        in_specs=[
            pl.BlockSpec((None, tq, 2 * HEAD_PAD), lambda b, hp, i: (b, i, hp)),
            pl.BlockSpec((None, seq, 2 * HEAD_PAD), lambda b, hp, i: (b, 0, hp)),
            pl.BlockSpec((None, seq, 2 * V_HEAD), lambda b, hp, i: (b, 0, hp)),
        ],
        out_specs=pl.BlockSpec((None, tq, 2 * V_HEAD), lambda b, hp, i: (b, i, hp)),
        out_shape=jax.ShapeDtypeStruct((nb, seq, N_HEADS * V_HEAD), BF16),
        compiler_params=_cparams(("parallel", "parallel", "arbitrary")),
        name="flash_prompt",
    )(q, k, v)


def _sattn_kernel(pt_ref, ql_ref, qr_ref, cn_ref, rn_ref, ckv_hbm, kr_hbm, o_ref, kvbuf, krbuf, sem,
                  *, layer, n_pages, chunk_pages, steps):
    b = pl.program_id(0)
    nb = pl.num_programs(0)
    slot = b % 2

    def page_copies(bb, sl, p):
        page = pt_ref[bb * n_pages + p]
        return (pltpu.make_async_copy(ckv_hbm.at[layer, page], kvbuf.at[sl, p], sem.at[0, sl]),
                pltpu.make_async_copy(kr_hbm.at[layer, page], krbuf.at[sl, p], sem.at[1, sl]))

    def start_fetch(bb, sl):
        def body(p, c):
            for cp in page_copies(bb, sl, p):
                cp.start()
            return c
        lax.fori_loop(0, n_pages, body, 0)

    def wait_fetch(bb, sl):
        def body(p, c):
            for cp in page_copies(bb, sl, p):
                cp.wait()
            return c
        lax.fori_loop(0, n_pages, body, 0)

    @pl.when(b == 0)
    def _():
        start_fetch(0, 0)

    @pl.when(b + 1 < nb)
    def _():
        start_fetch(b + 1, 1 - slot)

    wait_fetch(b, slot)

    ql = ql_ref[...]
    qr = qr_ref[...]
    rows = ql.shape[0]
    ck = chunk_pages * PAGE_SIZE

    def step(c, carry):
        m, l, acc = carry
        p0 = pl.multiple_of(c * chunk_pages, chunk_pages)
        kv = kvbuf[slot, pl.ds(p0, chunk_pages)].reshape(ck, KV_LORA).astype(BF16)
        kr = krbuf[slot, pl.ds(p0, chunk_pages)].reshape(ck, QK_ROPE).astype(BF16)
        s = _dot_t(ql, kv) + _dot_t(qr, kr)
        m_new = jnp.maximum(m, jnp.max(s, axis=-1, keepdims=True))
        alpha = jnp.exp(m - m_new)
        p = jnp.exp(s - m_new)
        l = alpha * l + jnp.sum(p, axis=-1, keepdims=True)
        acc = alpha * acc + _dot(p.astype(BF16), kv)
        return m_new, l, acc

    init = (jnp.full((rows, 1), -jnp.inf, F32), jnp.zeros((rows, 1), F32), jnp.zeros((rows, KV_LORA), F32))
    m, l, acc = lax.fori_loop(0, n_pages // chunk_pages, step, init)

    cn = cn_ref[...].astype(BF16)
    rn = rn_ref[...].astype(BF16)
    s = _dot_t(ql, cn) + _dot_t(qr, rn)
    qstep = lax.broadcasted_iota(jnp.int32, s.shape, 0) % steps
    kstep = lax.broadcasted_iota(jnp.int32, s.shape, 1)
    s = jnp.where(kstep <= qstep, s, -jnp.inf)
    m_new = jnp.maximum(m, jnp.max(s, axis=-1, keepdims=True))
    alpha = jnp.exp(m - m_new)
    p = jnp.exp(s - m_new)
    l = alpha * l + jnp.sum(p, axis=-1, keepdims=True)
    acc = alpha * acc + _dot(p.astype(BF16), cn)
    o_ref[...] = (acc / l).astype(o_ref.dtype)


def _sattn(page_table, ql, qr, cn, rn, cache_kv, cache_kr, *, layer, steps, chunk_pages):
    nb, rows, _ = ql.shape
    n_pages = page_table.shape[1]
    n_new = cn.shape[1]
    grid_spec = pltpu.PrefetchScalarGridSpec(
        num_scalar_prefetch=1,
        grid=(nb,),
        in_specs=[
            pl.BlockSpec((None, rows, KV_LORA), lambda b, pt: (b, 0, 0)),
            pl.BlockSpec((None, rows, QK_ROPE), lambda b, pt: (b, 0, 0)),
            pl.BlockSpec((None, n_new, KV_LORA), lambda b, pt: (b, 0, 0)),
            pl.BlockSpec((None, n_new, QK_ROPE), lambda b, pt: (b, 0, 0)),
            pl.BlockSpec(memory_space=pl.ANY),
            pl.BlockSpec(memory_space=pl.ANY),
        ],
        out_specs=pl.BlockSpec((None, rows, KV_LORA), lambda b, pt: (b, 0, 0)),
        scratch_shapes=[
            pltpu.VMEM((2, n_pages, PAGE_SIZE, KV_LORA), F32),
            pltpu.VMEM((2, n_pages, PAGE_SIZE, QK_ROPE), F32),
            pltpu.SemaphoreType.DMA((2, 2)),
        ],
    )
    return pl.pallas_call(
        functools.partial(_sattn_kernel, layer=layer, n_pages=n_pages, chunk_pages=chunk_pages, steps=steps),
        grid_spec=grid_spec,
        out_shape=jax.ShapeDtypeStruct((nb, rows, KV_LORA), BF16),
        compiler_params=_cparams(("arbitrary",)),
        name="sample_attend",
    )(page_table.reshape(-1), ql, qr, cn, rn, cache_kv, cache_kr)


def _mla_out_kernel(*refs, absorbed):
    if absorbed:
        o_ref, x_ref, mod_ref, npost_ref, bduv_ref, wo_ref, y_ref = refs
        o = _dot(_ld(o_ref), bduv_ref[...]).astype(BF16)
    else:
        o_ref, x_ref, mod_ref, npost_ref, wo_ref, y_ref = refs
        o = _ld(o_ref)
    f = _dot(o, wo_ref[...])
    x = _ld(x_ref)
    _st(y_ref, x + _mod(mod_ref, 2, x.shape[0]) * _rms(f, npost_ref[...]))


def _mla_out(o, x, mod, npost, wo, bduv=None, *, seq=None):
    r = _Rows(x, seq)
    in_specs = [r.spec(o.shape[-1]), r.spec(D_MODEL), r.mod_spec(), _resident((1, D_MODEL))]
    args = [o, x, mod, npost]
    if bduv is not None:
        in_specs.append(_resident(bduv.shape))
        args.append(bduv)
    in_specs.append(_resident(wo.shape))
    args.append(wo)
    return pl.pallas_call(
        functools.partial(_mla_out_kernel, absorbed=bduv is not None),
        grid=(r.grid,),
        in_specs=in_specs,
        out_specs=r.spec(D_MODEL),
        out_shape=r.shape(x, D_MODEL, F32),
        compiler_params=_cparams(("parallel",)),
        name="mla_out_sample" if bduv is not None else "mla_out_prompt",
    )(*args)


def _pool_finish(window_means, h, x, mod_ref, npost_ref, wg_ref, ls_ref, y_ref):
    outs = []
    for g in range(len(POOL_WINDOWS)):
        cols = slice(g * GROUP_W, (g + 1) * GROUP_W)
        pooled = (window_means[g] - h[:, cols]).astype(BF16)
        outs.append(_dot(pooled, wg_ref[g]))
    o = jnp.concatenate(outs, axis=-1) * ls_ref[...]
    _st(y_ref, x + _mod(mod_ref, 2, x.shape[0]) * _rms(o, npost_ref[...]))


def _pool_prompt_kernel(x_ref, xh_ref, mod_ref, npre_ref, npost_ref, wg_ref, ls_ref, y_ref, st_ref, hext_ref,
                        *, tm, tps):
    t_in_seq = pl.program_id(0) % tps
    keep = jnp.where(t_in_seq == 0, 0.0, 1.0)
    x = x_ref[...]
    h = _modulate(x, npre_ref[...], mod_ref[0], mod_ref[1])
    hh = _modulate(xh_ref[...], npre_ref[...], mod_ref[0], mod_ref[1]) * keep
    hext_ref[0:POOL_HALO, :] = hh
    hext_ref[POOL_HALO:, :] = h
    st_ref[...] = h[tm - POOL_HALO:, :]
    pos = t_in_seq * tm + lax.broadcasted_iota(jnp.int32, (tm, 1), 0)
    means = []
    for g, w in enumerate(POOL_WINDOWS):
        cols = slice(g * GROUP_W, (g + 1) * GROUP_W)
        s = h[:, cols]
        for j in range(1, w):
            s = s + hext_ref[POOL_HALO - j:POOL_HALO - j + tm, cols]
        means.append(s / jnp.minimum(pos + 1, w).astype(F32))
    _pool_finish(means, h, x, mod_ref, npost_ref, wg_ref, ls_ref, y_ref)


def _pool_prompt(x2d, mod, npre, npost, wg, ls, *, seq):
    r = _Rows(x2d, seq)
    hb = r.tm // POOL_HALO
    return pl.pallas_call(
        functools.partial(_pool_prompt_kernel, tm=r.tm, tps=r.tps),
        grid=(r.grid,),
        in_specs=[
            r.spec(D_MODEL),
            pl.BlockSpec((POOL_HALO, D_MODEL), lambda i: (jnp.maximum(i * hb - 1, 0), 0)),
            r.mod_spec(),
            _resident((1, D_MODEL)), _resident((1, D_MODEL)), _resident(wg.shape), _resident((1, D_MODEL)),
        ],
        out_specs=[r.spec(D_MODEL), pl.BlockSpec((None, POOL_HALO, D_MODEL), lambda i: (i // r.tps, 0, 0))],
        out_shape=[r.shape(x2d, D_MODEL, F32),
                   jax.ShapeDtypeStruct((x2d.shape[0] // seq, POOL_HALO, D_MODEL), F32)],
        scratch_shapes=[pltpu.VMEM((r.tm + POOL_HALO, D_MODEL), F32)],
        compiler_params=_cparams(("arbitrary",)),
        name="pool_prompt",
    )(x2d, x2d, mod, npre, npost, wg, ls)


def _pool_sample_kernel(x_ref, st_ref, mod_ref, npre_ref, npost_ref, wg_ref, ls_ref, y_ref, so_ref, *, past):
    steps, nbt, _ = x_ref.shape
    x = _ld(x_ref)
    h = _modulate(x, npre_ref[...], _mod(mod_ref, 0, x.shape[0]), _mod(mod_ref, 1, x.shape[0]))

    def slab(k, cols):
        if k < POOL_BUF:
            return st_ref[k, :, cols]
        return h[(k - POOL_BUF) * nbt:(k - POOL_BUF + 1) * nbt, cols]

    for k in range(POOL_BUF):
        so_ref[k] = slab(k + steps, slice(None))
    means = []
    for g, w in enumerate(POOL_WINDOWS):
        cols = slice(g * GROUP_W, (g + 1) * GROUP_W)
        parts = []
        for s_ in range(steps):
            acc = slab(POOL_BUF + s_, cols)
            for j in range(1, w):
                acc = acc + slab(POOL_BUF + s_ - j, cols)
            parts.append(acc / float(min(past + s_ + 1, w)))
        means.append(jnp.concatenate(parts, axis=0))
    _pool_finish(means, h, x, mod_ref, npost_ref, wg_ref, ls_ref, y_ref)


def _pool_sample(x3d, st3d, mod, npre, npost, wg, ls, *, past):
    r = _Rows(x3d)
    st_spec = pl.BlockSpec((POOL_BUF, r.nbt, D_MODEL), lambda i: (0, i, 0))
    return pl.pallas_call(
        functools.partial(_pool_sample_kernel, past=past),
        grid=(r.grid,),
        in_specs=[r.spec(D_MODEL), st_spec, r.mod_spec(),
                  _resident((1, D_MODEL)), _resident((1, D_MODEL)), _resident(wg.shape), _resident((1, D_MODEL))],
        out_specs=[r.spec(D_MODEL), st_spec],
        out_shape=[r.shape(x3d, D_MODEL, F32), jax.ShapeDtypeStruct(st3d.shape, F32)],
        compiler_params=_cparams(("parallel",)),
        name="pool_sample",
    )(x3d, st3d, mod, npre, npost, wg, ls)


def _ffn_chunks(h, prev_fn, cw_ref, cb_ref, wup_ref, wdn_ref, st_write):
    acc = None
    for c in range(D_FF // FF_CHUNK):
        halves = []
        for off in (0, D_FF):
            cols = slice(off + c * FF_CHUNK, off + (c + 1) * FF_CHUNK)
            u = _dot(h, wup_ref[:, cols])
            r1, r2 = prev_fn(u, cols)
            st_write(u, cols)
            cw = cw_ref[:, cols]
            halves.append(cb_ref[:, cols] + r2 * cw[0:1] + r1 * cw[1:2] + u * cw[2:3])
        gate = (_silu(halves[0]) * halves[1]).astype(BF16)
        part = _dot(gate, wdn_ref[c * FF_CHUNK:(c + 1) * FF_CHUNK, :])
        acc = part if acc is None else acc + part
    return acc


def _ffn_prompt_kernel(x_ref, xh_ref, mod_ref, npre_ref, npost_ref, wup_ref, cw_ref, cb_ref, wdn_ref,
                       y_ref, st_ref, *, tm, tps):
    keep = jnp.where(pl.program_id(0) % tps == 0, 0.0, 1.0)
    x = x_ref[...]
    h = _modulate(x, npre_ref[...], mod_ref[0], mod_ref[1]).astype(BF16)
    hh = _modulate(xh_ref[...], npre_ref[...], mod_ref[0], mod_ref[1]).astype(BF16)
    row = lax.broadcasted_iota(jnp.int32, (tm, 1), 0)

    def prev_fn(u, cols):
        up = _dot(hh, wup_ref[:, cols]) * keep
        last1 = up[CONV_HALO - 1:CONV_HALO]
        last2 = up[CONV_HALO - 2:CONV_HALO - 1]
        r1 = jnp.where(row == 0, last1, pltpu.roll(u, 1, 0))
        r2 = jnp.where(row == 0, last2, jnp.where(row == 1, last1, pltpu.roll(u, 2, 0)))
        return r1, r2

    def st_write(u, cols):
        st_ref[:, cols] = u[tm - CONV_HALO:, :]

    f = _ffn_chunks(h, prev_fn, cw_ref, cb_ref, wup_ref, wdn_ref, st_write)
    y_ref[...] = x + mod_ref[2] * _rms(f, npost_ref[...])


def _ffn_prompt(x2d, mod, npre, npost, wup, cw, cb, wdn, *, seq):
    r = _Rows(x2d, seq)
    hb = r.tm // CONV_HALO
    return pl.pallas_call(
        functools.partial(_ffn_prompt_kernel, tm=r.tm, tps=r.tps),
        grid=(r.grid,),
        in_specs=[
            r.spec(D_MODEL),
            pl.BlockSpec((CONV_HALO, D_MODEL), lambda i: (jnp.maximum(i * hb - 1, 0), 0)),
            r.mod_spec(),
            _resident((1, D_MODEL)), _resident((1, D_MODEL)),
            _resident(wup.shape), _resident(cw.shape), _resident(cb.shape), _resident(wdn.shape),
        ],
        out_specs=[r.spec(D_MODEL), pl.BlockSpec((None, CONV_HALO, 2 * D_FF), lambda i: (i // r.tps, 0, 0))],
        out_shape=[r.shape(x2d, D_MODEL, F32),
                   jax.ShapeDtypeStruct((x2d.shape[0] // seq, CONV_HALO, 2 * D_FF), F32)],
        compiler_params=_cparams(("arbitrary",)),
        name="ffn_prompt",
    )(x2d, x2d, mod, npre, npost, wup, cw, cb, wdn)


def _ffn_sample_kernel(x_ref, st_ref, mod_ref, npre_ref, npost_ref, wup_ref, cw_ref, cb_ref, wdn_ref,
                       y_ref, so_ref):
    steps, nbt, _ = x_ref.shape
    x = _ld(x_ref)
    rows = x.shape[0]
    h = _modulate(x, npre_ref[...], _mod(mod_ref, 0, rows), _mod(mod_ref, 1, rows)).astype(BF16)

    def prev_fn(u, cols):
        s0 = st_ref[0, :, cols]
        s1 = st_ref[1, :, cols]
        r1 = jnp.concatenate([s1, u[:rows - nbt]], axis=0)
        r2 = jnp.concatenate([s0, s1, u[:rows - 2 * nbt]], axis=0)
        return r1, r2

    def st_write(u, cols):
        so_ref[0, :, cols] = u[rows - 2 * nbt:rows - nbt, :]
        so_ref[1, :, cols] = u[rows - nbt:, :]

    f = _ffn_chunks(h, prev_fn, cw_ref, cb_ref, wup_ref, wdn_ref, st_write)
    _st(y_ref, x + _mod(mod_ref, 2, rows) * _rms(f, npost_ref[...]))


def _ffn_sample(x3d, st3d, mod, npre, npost, wup, cw, cb, wdn):
    r = _Rows(x3d)
    st_spec = pl.BlockSpec((CONV_BUF, r.nbt, 2 * D_FF), lambda i: (0, i, 0))
    return pl.pallas_call(
        _ffn_sample_kernel,
        grid=(r.grid,),
        in_specs=[r.spec(D_MODEL), st_spec, r.mod_spec(),
                  _resident((1, D_MODEL)), _resident((1, D_MODEL)),
                  _resident(wup.shape), _resident(cw.shape), _resident(cb.shape), _resident(wdn.shape)],
        out_specs=[r.spec(D_MODEL), st_spec],
        out_shape=[r.shape(x3d, D_MODEL, F32), jax.ShapeDtypeStruct(st3d.shape, F32)],
        compiler_params=_cparams(("parallel",)),
        name="ffn_sample",
    )(x3d, st3d, mod, npre, npost, wup, cw, cb, wdn)


def _rope_angles(pos):
    half = QK_ROPE // 2
    inv = ROPE_THETA ** (-jnp.arange(half, dtype=F32) / half)
    ang = pos.astype(F32)[:, None] * inv[None, :]
    return jnp.cos(ang), jnp.sin(ang)


def _rope_tables(pos, width):
    cos, sin = _rope_angles(pos)
    n = pos.shape[0]
    cos_t = jnp.concatenate([cos, cos, jnp.ones((n, width - QK_ROPE), F32)], axis=-1)
    sin_t = jnp.concatenate([-sin, sin, jnp.zeros((n, width - QK_ROPE), F32)], axis=-1)
    return cos_t, sin_t


def _swap_halves(w):
    half = QK_ROPE // 2
    return jnp.concatenate([w[..., half:], w[..., :half]], axis=-1)


def _mla_weights(w_dq, w_uq, w_dkv, w_uk, w_uv, w_o):
    wq3 = w_uq.reshape(Q_LORA, N_HEADS, QK_NOPE + QK_ROPE) * ATTN_SCALE
    nope, ropew = wq3[..., :QK_NOPE], wq3[..., QK_NOPE:]
    ropesw = _swap_halves(ropew)
    zpad = jnp.zeros((Q_LORA, N_HEADS, HEAD_PAD - QK_NOPE - QK_ROPE), F32)
    wqa_p = jnp.concatenate([ropew, nope, zpad], axis=-1).reshape(Q_LORA, N_HEADS * HEAD_PAD)
    wqb_p = jnp.concatenate([ropesw, jnp.zeros((Q_LORA, N_HEADS, HEAD_PAD - QK_ROPE), F32)], axis=-1)
    wqb_p = wqb_p.reshape(Q_LORA, N_HEADS * HEAD_PAD)
    kr_w = w_dkv[:, KV_LORA:]
    lane_pad = jnp.zeros((D_MODEL, HEAD_PAD - QK_ROPE), F32)
    wkv = jnp.concatenate([w_dkv[:, :KV_LORA], kr_w, lane_pad, _swap_halves(kr_w), lane_pad], axis=-1)
    eye_r = jnp.eye(QK_ROPE, dtype=F32)
    top = jnp.concatenate([jnp.zeros((KV_LORA, N_HEADS, QK_ROPE), F32), w_uk,
                           jnp.zeros((KV_LORA, N_HEADS, HEAD_PAD - QK_NOPE - QK_ROPE), F32)], axis=-1)
    mid = jnp.concatenate([jnp.broadcast_to(eye_r[:, None, :], (QK_ROPE, N_HEADS, QK_ROPE)),
                           jnp.zeros((QK_ROPE, N_HEADS, HEAD_PAD - QK_ROPE), F32)], axis=-1)
    bot = jnp.zeros((HEAD_PAD - QK_ROPE, N_HEADS, HEAD_PAD), F32)
    wk = jnp.concatenate([top, mid, bot], axis=0).reshape(KV_LORA + HEAD_PAD, N_HEADS * HEAD_PAD)
    wv = w_uv.reshape(KV_LORA, N_HEADS * V_HEAD)
    eye_h = jnp.eye(N_HEADS, dtype=F32)
    bduk = jnp.einsum('hg,khn->hngk', eye_h, w_uk).reshape(N_HEADS * QK_NOPE, N_HEADS * KV_LORA)
    bduv = jnp.einsum('hg,khv->hkgv', eye_h, w_uv).reshape(N_HEADS * KV_LORA, N_HEADS * V_HEAD)
    b = lambda a: a.astype(BF16)
    return dict(
        wdq=b(w_dq), wqa_p=b(wqa_p), wqb_p=b(wqb_p), wkv=b(wkv), wk=b(wk), wv=b(wv),
        wqn=b(nope.reshape(Q_LORA, N_HEADS * QK_NOPE)),
        wqa_s=b(ropew.reshape(Q_LORA, N_HEADS * QK_ROPE)), wqb_s=b(ropesw.reshape(Q_LORA, N_HEADS * QK_ROPE)),
        bduk=b(bduk), bduv=b(bduv), wo=b(w_o))


def kernel(x_prompt, x_sample, cache_kv_latent, cache_k_rope, state_pool, state_conv, page_table, c_prompt, c_sample, ada_w, ada_b, norm_mix_pre, norm_mix_post, norm_ffn_pre, norm_ffn_post, mla_w_dq, mla_q_norm, mla_w_uq, mla_w_dkv, mla_kv_norm, mla_w_uk, mla_w_uv, mla_w_o, pool_w, pool_scale, ffn_w_up, ffn_conv_w, ffn_conv_b, ffn_w_down):
    nbp, seq, d = x_prompt.shape
    nbs, steps, _ = x_sample.shape
    depth = ada_w.shape[0]
    past = page_table.shape[1] * PAGE_SIZE
    row = lambda a: a.reshape(1, -1)
    tmajor = lambda a: jnp.transpose(a, (1, 0, 2))

    c_all = jnp.concatenate([c_prompt, c_sample], axis=0)
    c_all = jnp.pad(c_all, ((0, (-c_all.shape[0]) % 8), (0, 0)))
    mods = _adaln(c_all, ada_w.astype(BF16), ada_b).reshape(depth, c_all.shape[0], 6, d)

    def mods_prompt(i, k0):
        return tmajor(mods[i, :nbp, k0:k0 + 3])[:, :, None, :]

    def mods_sample(i, k0):
        return tmajor(mods[i, nbp:nbp + nbs, k0:k0 + 3])

    xp = x_prompt.reshape(nbp * seq, d)
    xs = tmajor(x_sample)
    nbt = min(SAMPLE_TILE, nbs)

    outs = dict(kv_p=[], kr_p=[], kv_s=[], kr_s=[], pool_p=[], pool_s=[], conv_p=[], conv_s=[])
    for i in range(depth):
        j = i // 2
        if i % 2 == 0:
            w = _mla_weights(mla_w_dq[j], mla_w_uq[j], mla_w_dkv[j], mla_w_uk[j], mla_w_uv[j], mla_w_o[j])
            cos_p, sin_p = _rope_tables(jnp.arange(seq), HEAD_PAD)
            mp = mods_prompt(i, 0)
            q, k, v, ckv, kr = _proj_prompt(
                xp, mp, row(norm_mix_pre[i]), cos_p, sin_p, w['wdq'], row(mla_q_norm[j]), w['wqa_p'], w['wqb_p'],
                w['wkv'], row(mla_kv_norm[j]), w['wk'], w['wv'], seq=seq)
            hq = N_HEADS * HEAD_PAD
            o = _flash(q.reshape(nbp, seq, hq), k.reshape(nbp, seq, hq), v.reshape(nbp, seq, N_HEADS * V_HEAD))
            xp = _mla_out(o.reshape(nbp * seq, N_HEADS * V_HEAD), xp, mp, row(norm_mix_post[i]), w['wo'], seq=seq)
            outs['kv_p'].append(ckv.reshape(nbp, seq, KV_LORA))
            outs['kr_p'].append(kr.reshape(nbp, seq, QK_ROPE))
            cos_k, sin_k = _rope_tables(past + jnp.repeat(jnp.arange(steps), nbt), HEAD_PAD)
            cos_q = jnp.tile(cos_k[:, :QK_ROPE], (1, N_HEADS))
            sin_q = jnp.tile(sin_k[:, :QK_ROPE], (1, N_HEADS))
            ms = mods_sample(i, 0)
            ql, qr, ckv_s, kr_s = _proj_sample(
                xs, ms, row(norm_mix_pre[i]), cos_q, sin_q, cos_k, sin_k, w['wdq'], row(mla_q_norm[j]),
                w['wqn'], w['wqa_s'], w['wqb_s'], w['wkv'], row(mla_kv_norm[j]), w['bduk'])
            ql_b = jnp.transpose(ql.reshape(steps, nbs, N_HEADS, KV_LORA), (1, 2, 0, 3))
            qr_b = jnp.transpose(qr.reshape(steps, nbs, N_HEADS, QK_ROPE), (1, 2, 0, 3))
            ckv_sb = tmajor(ckv_s)
            kr_sb = tmajor(kr_s)
            key_pad = ((0, 0), (0, NEW_KEY_ROWS - steps), (0, 0))
            o_lat = _sattn(page_table, ql_b.reshape(nbs, N_HEADS * steps, KV_LORA),
                           qr_b.reshape(nbs, N_HEADS * steps, QK_ROPE),
                           jnp.pad(ckv_sb, key_pad), jnp.pad(kr_sb, key_pad),
                           cache_kv_latent, cache_k_rope, layer=j, steps=steps, chunk_pages=16)
            o_lat = jnp.transpose(o_lat.reshape(nbs, N_HEADS, steps, KV_LORA), (2, 0, 1, 3))
            xs = _mla_out(o_lat.reshape(steps, nbs, N_HEADS * KV_LORA), xs, ms, row(norm_mix_post[i]),
                          w['wo'], w['bduv'])
            outs['kv_s'].append(ckv_sb)
            outs['kr_s'].append(kr_sb)
        else:
            wg = pool_w[j].astype(BF16)
            xp, st_p = _pool_prompt(xp, mods_prompt(i, 0), row(norm_mix_pre[i]), row(norm_mix_post[i]), wg,
                                    row(pool_scale[j]), seq=seq)
            outs['pool_p'].append(st_p[:, POOL_HALO - POOL_BUF:])
            xs, st_s = _pool_sample(xs, tmajor(state_pool[j]), mods_sample(i, 0), row(norm_mix_pre[i]),
                                    row(norm_mix_post[i]), wg, row(pool_scale[j]), past=past)
            outs['pool_s'].append(tmajor(st_s))

        wup = ffn_w_up[i].astype(BF16)
        wdn = ffn_w_down[i].astype(BF16)
        xp, cst_p = _ffn_prompt(xp, mods_prompt(i, 3), row(norm_ffn_pre[i]), row(norm_ffn_post[i]), wup,
                                ffn_conv_w[i], row(ffn_conv_b[i]), wdn, seq=seq)
        outs['conv_p'].append(cst_p[:, CONV_HALO - CONV_BUF:])
        xs, cst_s = _ffn_sample(xs, tmajor(state_conv[i]), mods_sample(i, 3), row(norm_ffn_pre[i]),
                                row(norm_ffn_post[i]), wup, ffn_conv_w[i], row(ffn_conv_b[i]), wdn)
        outs['conv_s'].append(tmajor(cst_s))

    st = lambda key: jnp.stack(outs[key])
    return (xp.reshape(nbp, seq, d), tmajor(xs), st('kv_p'), st('kr_p'), st('kv_s'), st('kr_s'),
            st('pool_p'), st('pool_s'), st('conv_p'), st('conv_s'))
```

```python
import functools

import jax
import jax.numpy as jnp
from jax import lax
from jax.experimental import pallas as pl
from jax.experimental.pallas import tpu as pltpu

F32 = jnp.float32
BF16 = jnp.bfloat16

D_MODEL = 1024
N_HEADS = 16
QK_NOPE = 64
QK_ROPE = 32
V_HEAD = 64
Q_LORA = 512
KV_LORA = 256
ROPE_THETA = 10000.0
ATTN_SCALE = (QK_NOPE + QK_ROPE) ** -0.5
PAGE_SIZE = 128
POOL_WINDOWS = (2, 4, 8, 16)
GROUP_W = D_MODEL // len(POOL_WINDOWS)
POOL_BUF = 15
D_FF = 2816
CONV_W = 3
CONV_BUF = CONV_W - 1
EPS = 1e-6

HEAD_PAD = 128
POOL_HALO = 16
CONV_HALO = 8
FF_CHUNK = 256
NEW_KEY_ROWS = 16
PROMPT_TILE = 512
SAMPLE_TILE = 64
FLASH_TILE = 256
FLASH_HEADS = 4
VT_ROWS = 80
LOG2E = 1.4426950408889634
VMEM_LIMIT = 56 * 1024 * 1024


def _cparams(sem):
    return pltpu.CompilerParams(dimension_semantics=sem, vmem_limit_bytes=VMEM_LIMIT)


def _dot(a, b):
    return jnp.dot(a, b, preferred_element_type=F32)


def _dot_t(a, b):
    return lax.dot_general(a, b, (((1,), (1,)), ((), ())), preferred_element_type=F32)


def _rms(x, g):
    return x * lax.rsqrt(jnp.mean(x * x, axis=-1, keepdims=True) + EPS) * g


def _modulate(x, g, shift, scale):
    return _rms(x, g) * (1.0 + scale) + shift


def _silu(x):
    return x * (1.0 / (1.0 + jnp.exp(-x)))


def _resident(shape):
    zeros = (0,) * len(shape)
    return pl.BlockSpec(shape, lambda *_: zeros, pipeline_mode=pl.Buffered(1))


def _ld(ref):
    v = ref[...]
    return v if v.ndim == 2 else v.reshape(-1, v.shape[-1])


def _st(ref, val):
    ref[...] = val.reshape(ref.shape).astype(ref.dtype)


def _mod(mod_ref, k, rows):
    m = mod_ref[k]
    if m.shape[0] in (1, rows):
        return m
    return jnp.concatenate([m] * (rows // m.shape[0]), axis=0)


class _Rows:
    def __init__(self, x, seq=None):
        self.sample = x.ndim == 3
        if self.sample:
            self.steps, self.nb = x.shape[0], x.shape[1]
            self.nbt = min(SAMPLE_TILE, self.nb)
            self.grid = self.nb // self.nbt
            self.tile_rows = self.steps * self.nbt
        else:
            self.seq = seq
            self.tm = min(PROMPT_TILE, seq)
            self.grid = x.shape[0] // self.tm
            self.tps = seq // self.tm
            self.tile_rows = self.tm

    def spec(self, w):
        if self.sample:
            return pl.BlockSpec((self.steps, self.nbt, w), lambda i: (0, i, 0))
        return pl.BlockSpec((self.tm, w), lambda i: (i, 0))

    def shape(self, x, w, dtype):
        return jax.ShapeDtypeStruct(x.shape[:-1] + (w,), dtype)

    def mod_spec(self):
        if self.sample:
            return pl.BlockSpec((3, self.nbt, D_MODEL), lambda i: (0, i, 0))
        tps = self.tps
        return pl.BlockSpec((3, None, 1, D_MODEL), lambda i: (0, i // tps, 0, 0))

    def table_spec(self, w):
        if self.sample:
            return _resident((self.tile_rows, w))
        tps = self.tps
        return pl.BlockSpec((self.tm, w), lambda i: (i % tps, 0))


def _adaln_kernel(c_ref, w_ref, b_ref, o_ref):
    a = _silu(c_ref[...]).astype(BF16)
    o_ref[...] = _dot(a, w_ref[...].astype(BF16)) + b_ref[...]


def _adaln(c_all, w, b):
    depth, d, n = w.shape
    rows = c_all.shape[0]
    tn = 1536
    return pl.pallas_call(
        _adaln_kernel,
        grid=(depth, n // tn),
        in_specs=[
            pl.BlockSpec((rows, d), lambda i, j: (0, 0)),
            pl.BlockSpec((None, d, tn), lambda i, j: (i, 0, j)),
            pl.BlockSpec((None, 1, tn), lambda i, j: (i, 0, j)),
        ],
        out_specs=pl.BlockSpec((None, rows, tn), lambda i, j: (i, 0, j)),
        out_shape=jax.ShapeDtypeStruct((depth, rows, n), F32),
        compiler_params=_cparams(("parallel", "parallel")),
        name="adaln",
    )(c_all, w, b.reshape(depth, 1, n))


def _proj_common(x_ref, mod_ref, npre_ref, cosk, sink, wdq_ref, qn_ref, wkv_ref, kvn_ref, ckv_ref, kr_ref):
    x = _ld(x_ref)
    rows = x.shape[0]
    h = _modulate(x, npre_ref[...], _mod(mod_ref, 0, rows), _mod(mod_ref, 1, rows)).astype(BF16)
    cq = _rms(_dot(h, wdq_ref[...]), qn_ref[...]).astype(BF16)
    kv = _dot(h, wkv_ref[...])
    ckv = _rms(kv[:, :KV_LORA], kvn_ref[...])
    krp = kv[:, KV_LORA:KV_LORA + HEAD_PAD] * cosk + kv[:, KV_LORA + HEAD_PAD:] * sink
    _st(ckv_ref, ckv)
    _st(kr_ref, krp[:, :QK_ROPE])
    return cq, ckv, krp


def _proj_prompt_kernel(x_ref, mod_ref, npre_ref, cos_ref, sin_ref, wdq_ref, qn_ref, wqa_ref, wqb_ref,
                        wkv_ref, kvn_ref, wk_ref, wvt_ref, vbias_ref,
                        q_ref, k_ref, vt_ref, ckv_ref, kr_ref):
    cos = cos_ref[...]
    sin = sin_ref[...]
    cq, ckv, krp = _proj_common(x_ref, mod_ref, npre_ref, cos, sin, wdq_ref, qn_ref, wkv_ref, kvn_ref,
                                ckv_ref, kr_ref)
    qa = _dot(cq, wqa_ref[...])
    qb = _dot(cq, wqb_ref[...])
    for hd in range(N_HEADS):
        sl = slice(hd * HEAD_PAD, (hd + 1) * HEAD_PAD)
        q_ref[:, sl] = (qa[:, sl] * cos + qb[:, sl] * sin).astype(q_ref.dtype)
    ckv_b = ckv.astype(BF16)
    kcat = jnp.concatenate([ckv_b, krp.astype(BF16)], axis=-1)
    k_ref[...] = _dot(kcat, wk_ref[...]).astype(k_ref.dtype)
    vt = (_dot_t(wvt_ref[...], ckv_b) + vbias_ref[...]).astype(vt_ref.dtype)
    tk = vt_ref.shape[-1]
    for c in range(vt_ref.shape[0]):
        vt_ref[c] = vt[:, c * tk:(c + 1) * tk]


def _proj_prompt(x2d, mod, npre, cos_t, sin_t, wdq, qn, wqa, wqb, wkv, kvn, wk, wvt, vbias, *, seq):
    r = _Rows(x2d, seq)
    hq = N_HEADS * HEAD_PAD
    tk = min(FLASH_TILE, seq)
    kpt = r.tm // tk
    nbp = x2d.shape[0] // seq
    vt_rows = N_HEADS * VT_ROWS
    return pl.pallas_call(
        _proj_prompt_kernel,
        grid=(r.grid,),
        in_specs=[
            r.spec(D_MODEL), r.mod_spec(), _resident((1, D_MODEL)),
            r.table_spec(HEAD_PAD), r.table_spec(HEAD_PAD),
            _resident(wdq.shape), _resident(qn.shape), _resident(wqa.shape), _resident(wqb.shape),
            _resident(wkv.shape), _resident(kvn.shape), _resident(wk.shape), _resident(wvt.shape),
            _resident(vbias.shape),
        ],
        out_specs=[r.spec(hq), r.spec(hq),
                   pl.BlockSpec((None, kpt, vt_rows, tk), lambda i: (i // r.tps, i % r.tps, 0, 0)),
                   r.spec(KV_LORA), r.spec(QK_ROPE)],
        out_shape=[r.shape(x2d, hq, BF16), r.shape(x2d, hq, BF16),
                   jax.ShapeDtypeStruct((nbp, seq // tk, vt_rows, tk), BF16),
                   r.shape(x2d, KV_LORA, F32), r.shape(x2d, QK_ROPE, F32)],
        compiler_params=_cparams(("parallel",)),
        name="mla_proj_prompt",
    )(x2d, mod, npre, cos_t, sin_t, wdq, qn, wqa, wqb, wkv, kvn, wk, wvt, vbias)


def _proj_sample_kernel(x_ref, mod_ref, npre_ref, cosq_ref, sinq_ref, cosk_ref, sink_ref,
                        wdq_ref, qn_ref, wqn_ref, wqa_ref, wqb_ref, wkv_ref, kvn_ref, bduk_ref,
                        ql_ref, qr_ref, ckv_ref, kr_ref):
    cq, _, _ = _proj_common(x_ref, mod_ref, npre_ref, cosk_ref[...], sink_ref[...], wdq_ref, qn_ref, wkv_ref,
                            kvn_ref, ckv_ref, kr_ref)
    q_nope = _dot(cq, wqn_ref[...]).astype(BF16)
    _st(ql_ref, _dot(q_nope, bduk_ref[...]))
    _st(qr_ref, _dot(cq, wqa_ref[...]) * cosq_ref[...] + _dot(cq, wqb_ref[...]) * sinq_ref[...])


def _proj_sample(x3d, mod, npre, cosq, sinq, cosk, sink, wdq, qn, wqn, wqa, wqb, wkv, kvn, bduk):
    r = _Rows(x3d)
    return pl.pallas_call(
        _proj_sample_kernel,
        grid=(r.grid,),
        in_specs=[
            r.spec(D_MODEL), r.mod_spec(), _resident((1, D_MODEL)),
            r.table_spec(N_HEADS * QK_ROPE), r.table_spec(N_HEADS * QK_ROPE),
            r.table_spec(HEAD_PAD), r.table_spec(HEAD_PAD),
            _resident(wdq.shape), _resident(qn.shape), _resident(wqn.shape), _resident(wqa.shape),
            _resident(wqb.shape), _resident(wkv.shape), _resident(kvn.shape), _resident(bduk.shape),
        ],
        out_specs=[r.spec(N_HEADS * KV_LORA), r.spec(N_HEADS * QK_ROPE), r.spec(KV_LORA), r.spec(QK_ROPE)],
        out_shape=[r.shape(x3d, N_HEADS * KV_LORA, BF16), r.shape(x3d, N_HEADS * QK_ROPE, BF16),
                   r.shape(x3d, KV_LORA, F32), r.shape(x3d, QK_ROPE, F32)],
        compiler_params=_cparams(("parallel",)),
        name="mla_proj_sample",
    )(x3d, mod, npre, cosq, sinq, cosk, sink, wdq, qn, wqn, wqa, wqb, wkv, kvn, bduk)


def _flash_kernel(q_ref, k_ref, vt_ref, o_ref, sa_ref, sb_ref, *, tq, heads):
    qi = pl.program_id(2)
    krow = lax.broadcasted_iota(jnp.int32, (tq, tq), 0)
    qcol = lax.broadcasted_iota(jnp.int32, (tq, tq), 1)
    qs = [q_ref[:, h * HEAD_PAD:(h + 1) * HEAD_PAD] for h in range(heads)]

    def scores(ki, s_ref):
        k0 = pl.multiple_of(ki * tq, tq)
        for h in range(heads):
            s_ref[h] = _dot_t(k_ref[pl.ds(k0, tq), h * HEAD_PAD:(h + 1) * HEAD_PAD], qs[h])

    def consume(ki, s_ref, carry, masked):
        pts, ms, alphas = [], [], []
        for h in range(heads):
            st = s_ref[h]
            if masked:
                st = jnp.where(krow + (ki - qi) * tq <= qcol, st, -jnp.inf)
            m_new = jnp.maximum(carry[h][0], jnp.max(st, axis=0, keepdims=True))
            alphas.append(jnp.exp2(carry[h][0] - m_new))
            pts.append(jnp.exp2(st - m_new).astype(BF16))
            ms.append(m_new)
        new = []
        for h in range(heads):
            pv = _dot(vt_ref[ki, h * VT_ROWS:(h + 1) * VT_ROWS, :], pts[h])
            new.append((ms[h], alphas[h] * carry[h][1] + pv))
        return tuple(new)

    def pair(j, carry, masked):
        scores(2 * j + 1, sb_ref)
        carry = consume(2 * j, sa_ref, carry, masked)
        if not masked:
            scores(2 * j + 2, sa_ref)
        return consume(2 * j + 1, sb_ref, carry, masked)

    scores(0, sa_ref)
    init = tuple((jnp.full((1, tq), -jnp.inf, F32), jnp.zeros((VT_ROWS, tq), F32)) for _ in range(heads))
    carry = lax.fori_loop(0, qi // 2, functools.partial(pair, masked=False), init)
    carry = pair(qi // 2, carry, True)
    for h in range(heads):
        acc = carry[h][1]
        o_ref[h * V_HEAD:(h + 1) * V_HEAD, :] = (acc[:V_HEAD] / acc[V_HEAD:V_HEAD + 1]).astype(o_ref.dtype)


def _flash(q, k, vt):
    nb, seq, _ = q.shape
    tq = vt.shape[-1]
    heads = FLASH_HEADS
    assert (seq // tq) % 2 == 0 and N_HEADS % heads == 0
    return pl.pallas_call(
        functools.partial(_flash_kernel, tq=tq, heads=heads),
        grid=(nb, N_HEADS // heads, seq // tq),
        in_specs=[
            pl.BlockSpec((None, tq, heads * HEAD_PAD), lambda b, g, i: (b, i, g)),
            pl.BlockSpec((None, seq, heads * HEAD_PAD), lambda b, g, i: (b, 0, g)),
            pl.BlockSpec((None, seq // tq, heads * VT_ROWS, tq), lambda b, g, i: (b, 0, g, 0)),
        ],
        out_specs=pl.BlockSpec((None, heads * V_HEAD, tq), lambda b, g, i: (b, g, i)),
        out_shape=jax.ShapeDtypeStruct((nb, N_HEADS * V_HEAD, seq), BF16),
        scratch_shapes=[pltpu.VMEM((heads, tq, tq), F32), pltpu.VMEM((heads, tq, tq), F32)],
        compiler_params=_cparams(("parallel", "parallel", "arbitrary")),
        name="flash_prompt",
    )(q, k, vt)


def _sattn_kernel(pt_ref, ql_ref, qr_ref, cn_ref, rn_ref, ckv_hbm, krt_hbm, o_ref, kvbuf, krbuf, sem,
                  s_a, s_b, kv_a, kv_b, *, layer, n_pages, chunk_pages, steps):
    b = pl.program_id(0)
    nb = pl.num_programs(0)
    slot = b % 2

    def page_copies(bb, sl, p):
        page = pt_ref[bb * n_pages + p]
        return (pltpu.make_async_copy(ckv_hbm.at[layer, page], kvbuf.at[sl, p], sem.at[0, sl]),
                pltpu.make_async_copy(krt_hbm.at[layer, page], krbuf.at[sl, p], sem.at[1, sl]))

    def start_fetch(bb, sl):
        def body(p, c):
            for cp in page_copies(bb, sl, p):
                cp.start()
            return c
        lax.fori_loop(0, n_pages, body, 0, unroll=8)

    def wait_fetch(bb, sl):
        def body(p, c):
            for cp in page_copies(bb, sl, p):
                cp.wait()
            return c
        lax.fori_loop(0, n_pages, body, 0, unroll=8)

    @pl.when(b == 0)
    def _():
        start_fetch(0, 0)

    @pl.when(b + 1 < nb)
    def _():
        start_fetch(b + 1, 1 - slot)

    wait_fetch(b, slot)

    ql = ql_ref[...]
    qr = qr_ref[...]
    rows = ql.shape[0]
    ck = chunk_pages * PAGE_SIZE

    s_bufs = (s_a, s_b)
    kv_bufs = (kv_a, kv_b)

    def scores(c):
        p0 = c * chunk_pages
        kv = kvbuf[slot, p0:p0 + chunk_pages].reshape(ck, KV_LORA).astype(BF16)
        kv_bufs[c % 2][...] = kv
        krt = jnp.concatenate([krbuf[slot, p0 + i] for i in range(chunk_pages)], axis=1).astype(BF16)
        s_bufs[c % 2][...] = _dot_t(ql, kv) + _dot(qr, krt)

    def softmax_step(s, values, carry):
        m, l, acc = carry
        m_new = jnp.maximum(m, jnp.max(s, axis=-1, keepdims=True))
        alpha = jnp.exp(m - m_new)
        p = jnp.exp(s - m_new)
        l = alpha * l + jnp.sum(p, axis=-1, keepdims=True)
        return m_new, l, alpha * acc + _dot(p.astype(BF16), values)

    carry = (jnp.full((rows, 1), -jnp.inf, F32), jnp.zeros((rows, 1), F32), jnp.zeros((rows, KV_LORA), F32))
    n_chunks = n_pages // chunk_pages
    scores(0)
    for c in range(n_chunks):
        if c + 1 < n_chunks:
            scores(c + 1)
        carry = softmax_step(s_bufs[c % 2][...], kv_bufs[c % 2][...], carry)

    cn = cn_ref[...].astype(BF16)
    rn = rn_ref[...].astype(BF16)
    s = _dot_t(ql, cn) + _dot_t(qr, rn)
    qstep = lax.broadcasted_iota(jnp.int32, s.shape, 0) % steps
    kstep = lax.broadcasted_iota(jnp.int32, s.shape, 1)
    s = jnp.where(kstep <= qstep, s, -jnp.inf)
    _, l, acc = softmax_step(s, cn, carry)
    o_ref[...] = (acc / l).astype(o_ref.dtype)


def _sattn(page_table, ql, qr, cn, rn, cache_kv, cache_krt, *, layer, steps, chunk_pages):
    nb, rows, _ = ql.shape
    n_pages = page_table.shape[1]
    n_new = cn.shape[1]
    ck = chunk_pages * PAGE_SIZE
    grid_spec = pltpu.PrefetchScalarGridSpec(
        num_scalar_prefetch=1,
        grid=(nb,),
        in_specs=[
            pl.BlockSpec((None, rows, KV_LORA), lambda b, pt: (b, 0, 0)),
            pl.BlockSpec((None, rows, QK_ROPE), lambda b, pt: (b, 0, 0)),
            pl.BlockSpec((None, n_new, KV_LORA), lambda b, pt: (b, 0, 0)),
            pl.BlockSpec((None, n_new, QK_ROPE), lambda b, pt: (b, 0, 0)),
            pl.BlockSpec(memory_space=pl.ANY),
            pl.BlockSpec(memory_space=pl.ANY),
        ],
        out_specs=pl.BlockSpec((None, rows, KV_LORA), lambda b, pt: (b, 0, 0)),
        scratch_shapes=[
            pltpu.VMEM((2, n_pages, PAGE_SIZE, KV_LORA), F32),
            pltpu.VMEM((2, n_pages, QK_ROPE, PAGE_SIZE), F32),
            pltpu.SemaphoreType.DMA((2, 2)),
            pltpu.VMEM((rows, ck), F32), pltpu.VMEM((rows, ck), F32),
            pltpu.VMEM((ck, KV_LORA), BF16), pltpu.VMEM((ck, KV_LORA), BF16),
        ],
    )
    return pl.pallas_call(
        functools.partial(_sattn_kernel, layer=layer, n_pages=n_pages, chunk_pages=chunk_pages, steps=steps),
        grid_spec=grid_spec,
        out_shape=jax.ShapeDtypeStruct((nb, rows, KV_LORA), BF16),
        compiler_params=_cparams(("arbitrary",)),
        name="sample_attend",
    )(page_table.reshape(-1), ql, qr, cn, rn, cache_kv, cache_krt)


def _mla_out_kernel(*refs, absorbed):
    if absorbed:
        o_ref, x_ref, mod_ref, npost_ref, bduv_ref, wo_ref, y_ref = refs
        o = _dot(_ld(o_ref), bduv_ref[...]).astype(BF16)
        f = _dot(o, wo_ref[...])
    else:
        o_ref, x_ref, mod_ref, npost_ref, wo_ref, y_ref = refs
        f = lax.dot_general(o_ref[...], wo_ref[...], (((0,), (0,)), ((), ())), preferred_element_type=F32)
    x = _ld(x_ref)
    _st(y_ref, x + _mod(mod_ref, 2, x.shape[0]) * _rms(f, npost_ref[...]))


def _mla_out(o, x, mod, npost, wo, bduv=None, *, seq=None):
    r = _Rows(x, seq)
    if bduv is not None:
        o_spec = r.spec(o.shape[-1])
    else:
        o_spec = pl.BlockSpec((None, o.shape[1], r.tm), lambda i: (i // r.tps, 0, i % r.tps))
    in_specs = [o_spec, r.spec(D_MODEL), r.mod_spec(), _resident((1, D_MODEL))]
    args = [o, x, mod, npost]
    if bduv is not None:
        in_specs.append(_resident(bduv.shape))
        args.append(bduv)
    in_specs.append(_resident(wo.shape))
    args.append(wo)
    return pl.pallas_call(
        functools.partial(_mla_out_kernel, absorbed=bduv is not None),
        grid=(r.grid,),
        in_specs=in_specs,
        out_specs=r.spec(D_MODEL),
        out_shape=r.shape(x, D_MODEL, F32),
        compiler_params=_cparams(("parallel",)),
        name="mla_out_sample" if bduv is not None else "mla_out_prompt",
    )(*args)


def _pool_finish(window_means, h, x, mod_ref, npost_ref, wg_ref, ls_ref, y_ref):
    outs = []
    for g in range(len(POOL_WINDOWS)):
        cols = slice(g * GROUP_W, (g + 1) * GROUP_W)
        pooled = (window_means[g] - h[:, cols]).astype(BF16)
        outs.append(_dot(pooled, wg_ref[g]))
    o = jnp.concatenate(outs, axis=-1) * ls_ref[...]
    _st(y_ref, x + _mod(mod_ref, 2, x.shape[0]) * _rms(o, npost_ref[...]))


def _pool_prompt_kernel(x_ref, xh_ref, mod_ref, npre_ref, npost_ref, wg_ref, ls_ref, y_ref, st_ref, hext_ref,
                        *, tm, tps):
    t_in_seq = pl.program_id(0) % tps
    keep = jnp.where(t_in_seq == 0, 0.0, 1.0)
    x = x_ref[...]
    h = _modulate(x, npre_ref[...], mod_ref[0], mod_ref[1])
    hh = _modulate(xh_ref[...], npre_ref[...], mod_ref[0], mod_ref[1]) * keep
    hext_ref[0:POOL_HALO, :] = hh
    hext_ref[POOL_HALO:, :] = h
    st_ref[...] = h[tm - POOL_HALO:, :]
    pos = t_in_seq * tm + lax.broadcasted_iota(jnp.int32, (tm, 1), 0)
    means = []
    for g, w in enumerate(POOL_WINDOWS):
        cols = slice(g * GROUP_W, (g + 1) * GROUP_W)
        s = h[:, cols]
        for j in range(1, w):
            s = s + hext_ref[POOL_HALO - j:POOL_HALO - j + tm, cols]
        means.append(s / jnp.minimum(pos + 1, w).astype(F32))
    _pool_finish(means, h, x, mod_ref, npost_ref, wg_ref, ls_ref, y_ref)


def _pool_prompt(x2d, mod, npre, npost, wg, ls, *, seq):
    r = _Rows(x2d, seq)
    hb = r.tm // POOL_HALO
    return pl.pallas_call(
        functools.partial(_pool_prompt_kernel, tm=r.tm, tps=r.tps),
        grid=(r.grid,),
        in_specs=[
            r.spec(D_MODEL),
            pl.BlockSpec((POOL_HALO, D_MODEL), lambda i: (jnp.maximum(i * hb - 1, 0), 0)),
            r.mod_spec(),
            _resident((1, D_MODEL)), _resident((1, D_MODEL)), _resident(wg.shape), _resident((1, D_MODEL)),
        ],
        out_specs=[r.spec(D_MODEL), pl.BlockSpec((None, POOL_HALO, D_MODEL), lambda i: (i // r.tps, 0, 0))],
        out_shape=[r.shape(x2d, D_MODEL, F32),
                   jax.ShapeDtypeStruct((x2d.shape[0] // seq, POOL_HALO, D_MODEL), F32)],
        scratch_shapes=[pltpu.VMEM((r.tm + POOL_HALO, D_MODEL), F32)],
        compiler_params=_cparams(("arbitrary",)),
        name="pool_prompt",
    )(x2d, x2d, mod, npre, npost, wg, ls)


def _pool_sample_kernel(x_ref, st_ref, mod_ref, npre_ref, npost_ref, wg_ref, ls_ref, y_ref, so_ref, *, past):
    steps, nbt, _ = x_ref.shape
    x = _ld(x_ref)
    h = _modulate(x, npre_ref[...], _mod(mod_ref, 0, x.shape[0]), _mod(mod_ref, 1, x.shape[0]))

    def slab(k, cols):
        if k < POOL_BUF:
            return st_ref[k, :, cols]
        return h[(k - POOL_BUF) * nbt:(k - POOL_BUF + 1) * nbt, cols]

    for k in range(POOL_BUF):
        so_ref[k] = slab(k + steps, slice(None))
    means = []
    for g, w in enumerate(POOL_WINDOWS):
        cols = slice(g * GROUP_W, (g + 1) * GROUP_W)
        parts = []
        for s_ in range(steps):
            acc = slab(POOL_BUF + s_, cols)
            for j in range(1, w):
                acc = acc + slab(POOL_BUF + s_ - j, cols)
            parts.append(acc / float(min(past + s_ + 1, w)))
        means.append(jnp.concatenate(parts, axis=0))
    _pool_finish(means, h, x, mod_ref, npost_ref, wg_ref, ls_ref, y_ref)


def _pool_sample(x3d, st3d, mod, npre, npost, wg, ls, *, past):
    r = _Rows(x3d)
    st_spec = pl.BlockSpec((POOL_BUF, r.nbt, D_MODEL), lambda i: (0, i, 0))
    return pl.pallas_call(
        functools.partial(_pool_sample_kernel, past=past),
        grid=(r.grid,),
        in_specs=[r.spec(D_MODEL), st_spec, r.mod_spec(),
                  _resident((1, D_MODEL)), _resident((1, D_MODEL)), _resident(wg.shape), _resident((1, D_MODEL))],
        out_specs=[r.spec(D_MODEL), st_spec],
        out_shape=[r.shape(x3d, D_MODEL, F32), jax.ShapeDtypeStruct(st3d.shape, F32)],
        compiler_params=_cparams(("parallel",)),
        name="pool_sample",
    )(x3d, st3d, mod, npre, npost, wg, ls)


def _ffn_chunks(h, prev_fn, cw_ref, cb_ref, wup_ref, wdn_ref, st_write):
    acc = None
    for c in range(D_FF // FF_CHUNK):
        halves = []
        for off in (0, D_FF):
            cols = slice(off + c * FF_CHUNK, off + (c + 1) * FF_CHUNK)
            u = _dot(h, wup_ref[:, cols])
            r1, r2 = prev_fn(u, cols)
            st_write(u, cols)
            cw = cw_ref[:, cols]
            halves.append(cb_ref[:, cols] + r2 * cw[0:1] + r1 * cw[1:2] + u * cw[2:3])
        gate = (_silu(halves[0]) * halves[1]).astype(BF16)
        part = _dot(gate, wdn_ref[c * FF_CHUNK:(c + 1) * FF_CHUNK, :])
        acc = part if acc is None else acc + part
    return acc


def _ffn_prompt_kernel(x_ref, xh_ref, mod_ref, npre_ref, npost_ref, wup_ref, cw_ref, cb_ref, wdn_ref,
                       y_ref, st_ref, *, tm, tps):
    keep = jnp.where(pl.program_id(0) % tps == 0, 0.0, 1.0)
    x = x_ref[...]
    h = _modulate(x, npre_ref[...], mod_ref[0], mod_ref[1]).astype(BF16)
    hh = _modulate(xh_ref[...], npre_ref[...], mod_ref[0], mod_ref[1]).astype(BF16)
    row = lax.broadcasted_iota(jnp.int32, (tm, 1), 0)

    def prev_fn(u, cols):
        up = _dot(hh, wup_ref[:, cols]) * keep
        last1 = up[CONV_HALO - 1:CONV_HALO]
        last2 = up[CONV_HALO - 2:CONV_HALO - 1]
        r1 = jnp.where(row == 0, last1, pltpu.roll(u, 1, 0))
        r2 = jnp.where(row == 0, last2, jnp.where(row == 1, last1, pltpu.roll(u, 2, 0)))
        return r1, r2

    def st_write(u, cols):
        st_ref[:, cols] = u[tm - CONV_HALO:, :]

    f = _ffn_chunks(h, prev_fn, cw_ref, cb_ref, wup_ref, wdn_ref, st_write)
    y_ref[...] = x + mod_ref[2] * _rms(f, npost_ref[...])


def _ffn_prompt(x2d, mod, npre, npost, wup, cw, cb, wdn, *, seq):
    r = _Rows(x2d, seq)
    hb = r.tm // CONV_HALO
    return pl.pallas_call(
        functools.partial(_ffn_prompt_kernel, tm=r.tm, tps=r.tps),
        grid=(r.grid,),
        in_specs=[
            r.spec(D_MODEL),
            pl.BlockSpec((CONV_HALO, D_MODEL), lambda i: (jnp.maximum(i * hb - 1, 0), 0)),
            r.mod_spec(),
            _resident((1, D_MODEL)), _resident((1, D_MODEL)),
            _resident(wup.shape), _resident(cw.shape), _resident(cb.shape), _resident(wdn.shape),
        ],
        out_specs=[r.spec(D_MODEL), pl.BlockSpec((None, CONV_HALO, 2 * D_FF), lambda i: (i // r.tps, 0, 0))],
        out_shape=[r.shape(x2d, D_MODEL, F32),
                   jax.ShapeDtypeStruct((x2d.shape[0] // seq, CONV_HALO, 2 * D_FF), F32)],
        compiler_params=_cparams(("arbitrary",)),
        name="ffn_prompt",
    )(x2d, x2d, mod, npre, npost, wup, cw, cb, wdn)


def _ffn_sample_kernel(x_ref, st_ref, mod_ref, npre_ref, npost_ref, wup_ref, cw_ref, cb_ref, wdn_ref,
                       y_ref, so_ref):
    steps, nbt, _ = x_ref.shape
    x = _ld(x_ref)
    rows = x.shape[0]
    h = _modulate(x, npre_ref[...], _mod(mod_ref, 0, rows), _mod(mod_ref, 1, rows)).astype(BF16)

    def prev_fn(u, cols):
        s0 = st_ref[0, :, cols]
        s1 = st_ref[1, :, cols]
        r1 = jnp.concatenate([s1, u[:rows - nbt]], axis=0)
        r2 = jnp.concatenate([s0, s1, u[:rows - 2 * nbt]], axis=0)
        return r1, r2

    def st_write(u, cols):
        so_ref[0, :, cols] = u[rows - 2 * nbt:rows - nbt, :]
        so_ref[1, :, cols] = u[rows - nbt:, :]

    f = _ffn_chunks(h, prev_fn, cw_ref, cb_ref, wup_ref, wdn_ref, st_write)
    _st(y_ref, x + _mod(mod_ref, 2, rows) * _rms(f, npost_ref[...]))


def _ffn_sample(x3d, st3d, mod, npre, npost, wup, cw, cb, wdn):
    r = _Rows(x3d)
    st_spec = pl.BlockSpec((CONV_BUF, r.nbt, 2 * D_FF), lambda i: (0, i, 0))
    return pl.pallas_call(
        _ffn_sample_kernel,
        grid=(r.grid,),
        in_specs=[r.spec(D_MODEL), st_spec, r.mod_spec(),
                  _resident((1, D_MODEL)), _resident((1, D_MODEL)),
                  _resident(wup.shape), _resident(cw.shape), _resident(cb.shape), _resident(wdn.shape)],
        out_specs=[r.spec(D_MODEL), st_spec],
        out_shape=[r.shape(x3d, D_MODEL, F32), jax.ShapeDtypeStruct(st3d.shape, F32)],
        compiler_params=_cparams(("parallel",)),
        name="ffn_sample",
    )(x3d, st3d, mod, npre, npost, wup, cw, cb, wdn)


def _rope_angles(pos):
    half = QK_ROPE // 2
    inv = ROPE_THETA ** (-jnp.arange(half, dtype=F32) / half)
    ang = pos.astype(F32)[:, None] * inv[None, :]
    return jnp.cos(ang), jnp.sin(ang)


def _rope_tables(pos, width):
    cos, sin = _rope_angles(pos)
    n = pos.shape[0]
    cos_t = jnp.concatenate([cos, cos, jnp.ones((n, width - QK_ROPE), F32)], axis=-1)
    sin_t = jnp.concatenate([-sin, sin, jnp.zeros((n, width - QK_ROPE), F32)], axis=-1)
    return cos_t, sin_t


def _swap_halves(w):
    half = QK_ROPE // 2
    return jnp.concatenate([w[..., half:], w[..., :half]], axis=-1)


def _mla_weights(w_dq, w_uq, w_dkv, w_uk, w_uv, w_o):
    wq3 = w_uq.reshape(Q_LORA, N_HEADS, QK_NOPE + QK_ROPE) * ATTN_SCALE
    nope, ropew = wq3[..., :QK_NOPE], wq3[..., QK_NOPE:]
    ropesw = _swap_halves(ropew)
    zpad = jnp.zeros((Q_LORA, N_HEADS, HEAD_PAD - QK_NOPE - QK_ROPE), F32)
    wqa_p = jnp.concatenate([ropew, nope, zpad], axis=-1).reshape(Q_LORA, N_HEADS * HEAD_PAD)
    wqb_p = jnp.concatenate([ropesw, jnp.zeros((Q_LORA, N_HEADS, HEAD_PAD - QK_ROPE), F32)], axis=-1)
    wqb_p = wqb_p.reshape(Q_LORA, N_HEADS * HEAD_PAD)
    kr_w = w_dkv[:, KV_LORA:]
    lane_pad = jnp.zeros((D_MODEL, HEAD_PAD - QK_ROPE), F32)
    wkv = jnp.concatenate([w_dkv[:, :KV_LORA], kr_w, lane_pad, _swap_halves(kr_w), lane_pad], axis=-1)
    eye_r = jnp.eye(QK_ROPE, dtype=F32)
    top = jnp.concatenate([jnp.zeros((KV_LORA, N_HEADS, QK_ROPE), F32), w_uk,
                           jnp.zeros((KV_LORA, N_HEADS, HEAD_PAD - QK_NOPE - QK_ROPE), F32)], axis=-1)
    mid = jnp.concatenate([jnp.broadcast_to(eye_r[:, None, :], (QK_ROPE, N_HEADS, QK_ROPE)),
                           jnp.zeros((QK_ROPE, N_HEADS, HEAD_PAD - QK_ROPE), F32)], axis=-1)
    bot = jnp.zeros((HEAD_PAD - QK_ROPE, N_HEADS, HEAD_PAD), F32)
    wk = jnp.concatenate([top, mid, bot], axis=0).reshape(KV_LORA + HEAD_PAD, N_HEADS * HEAD_PAD)
    wvt = jnp.pad(jnp.transpose(w_uv, (1, 2, 0)), ((0, 0), (0, VT_ROWS - V_HEAD), (0, 0)))
    wvt = wvt.reshape(N_HEADS * VT_ROWS, KV_LORA)
    vbias = jnp.tile((jnp.arange(VT_ROWS) == V_HEAD).astype(F32), N_HEADS).reshape(N_HEADS * VT_ROWS, 1)
    eye_h = jnp.eye(N_HEADS, dtype=F32)
    bduk = jnp.einsum('hg,khn->hngk', eye_h, w_uk).reshape(N_HEADS * QK_NOPE, N_HEADS * KV_LORA)
    bduv = jnp.einsum('hg,khv->hkgv', eye_h, w_uv).reshape(N_HEADS * KV_LORA, N_HEADS * V_HEAD)
    b = lambda a: a.astype(BF16)
    return dict(
        wdq=b(w_dq), wqa_p=b(wqa_p * LOG2E), wqb_p=b(wqb_p * LOG2E), wkv=b(wkv), wk=b(wk), wvt=b(wvt), vbias=vbias,
        wqn=b(nope.reshape(Q_LORA, N_HEADS * QK_NOPE)),
        wqa_s=b(ropew.reshape(Q_LORA, N_HEADS * QK_ROPE)), wqb_s=b(ropesw.reshape(Q_LORA, N_HEADS * QK_ROPE)),
        bduk=b(bduk), bduv=b(bduv), wo=b(w_o))


def kernel(x_prompt, x_sample, cache_kv_latent, cache_k_rope, state_pool, state_conv, page_table, c_prompt, c_sample, ada_w, ada_b, norm_mix_pre, norm_mix_post, norm_ffn_pre, norm_ffn_post, mla_w_dq, mla_q_norm, mla_w_uq, mla_w_dkv, mla_kv_norm, mla_w_uk, mla_w_uv, mla_w_o, pool_w, pool_scale, ffn_w_up, ffn_conv_w, ffn_conv_b, ffn_w_down):
    nbp, seq, d = x_prompt.shape
    nbs, steps, _ = x_sample.shape
    depth = ada_w.shape[0]
    past = page_table.shape[1] * PAGE_SIZE
    row = lambda a: a.reshape(1, -1)
    tmajor = lambda a: jnp.transpose(a, (1, 0, 2))

    c_all = jnp.concatenate([c_prompt, c_sample], axis=0)
    c_all = jnp.pad(c_all, ((0, (-c_all.shape[0]) % 8), (0, 0)))
    mods = _adaln(c_all, ada_w, ada_b).reshape(depth, c_all.shape[0], 6, d)

    def mods_prompt(i, k0):
        return tmajor(mods[i, :nbp, k0:k0 + 3])[:, :, None, :]

    def mods_sample(i, k0):
        return tmajor(mods[i, nbp:nbp + nbs, k0:k0 + 3])

    xp = x_prompt.reshape(nbp * seq, d)
    xs = tmajor(x_sample)
    nbt = min(SAMPLE_TILE, nbs)

    outs = dict(kv_p=[], kr_p=[], kv_s=[], kr_s=[], pool_p=[], pool_s=[], conv_p=[], conv_s=[])
    for i in range(depth):
        j = i // 2
        if i % 2 == 0:
            w = _mla_weights(mla_w_dq[j], mla_w_uq[j], mla_w_dkv[j], mla_w_uk[j], mla_w_uv[j], mla_w_o[j])
            cos_p, sin_p = _rope_tables(jnp.arange(seq), HEAD_PAD)
            mp = mods_prompt(i, 0)
            q, k, vt, ckv, kr = _proj_prompt(
                xp, mp, row(norm_mix_pre[i]), cos_p, sin_p, w['wdq'], row(mla_q_norm[j]), w['wqa_p'], w['wqb_p'],
                w['wkv'], row(mla_kv_norm[j]), w['wk'], w['wvt'], w['vbias'], seq=seq)
            hq = N_HEADS * HEAD_PAD
            o_t = _flash(q.reshape(nbp, seq, hq), k.reshape(nbp, seq, hq), vt)
            xp = _mla_out(o_t, xp, mp, row(norm_mix_post[i]), w['wo'], seq=seq)
            outs['kv_p'].append(ckv.reshape(nbp, seq, KV_LORA))
            outs['kr_p'].append(kr.reshape(nbp, seq, QK_ROPE))
            cos_k, sin_k = _rope_tables(past + jnp.repeat(jnp.arange(steps), nbt), HEAD_PAD)
            cos_q = jnp.tile(cos_k[:, :QK_ROPE], (1, N_HEADS))
            sin_q = jnp.tile(sin_k[:, :QK_ROPE], (1, N_HEADS))
            ms = mods_sample(i, 0)
            ql, qr, ckv_s, kr_s = _proj_sample(
                xs, ms, row(norm_mix_pre[i]), cos_q, sin_q, cos_k, sin_k, w['wdq'], row(mla_q_norm[j]),
                w['wqn'], w['wqa_s'], w['wqb_s'], w['wkv'], row(mla_kv_norm[j]), w['bduk'])
            ql_b = jnp.transpose(ql.reshape(steps, nbs, N_HEADS, KV_LORA), (1, 2, 0, 3))
            qr_b = jnp.transpose(qr.reshape(steps, nbs, N_HEADS, QK_ROPE), (1, 2, 0, 3))
            ckv_sb = tmajor(ckv_s)
            kr_sb = tmajor(kr_s)
            key_pad = ((0, 0), (0, NEW_KEY_ROWS - steps), (0, 0))
            o_lat = _sattn(page_table, ql_b.reshape(nbs, N_HEADS * steps, KV_LORA),
                           qr_b.reshape(nbs, N_HEADS * steps, QK_ROPE),
                           jnp.pad(ckv_sb, key_pad), jnp.pad(kr_sb, key_pad),
                           cache_kv_latent, jnp.swapaxes(cache_k_rope, 2, 3), layer=j, steps=steps,
                           chunk_pages=16)
            o_lat = jnp.transpose(o_lat.reshape(nbs, N_HEADS, steps, KV_LORA), (2, 0, 1, 3))
            xs = _mla_out(o_lat.reshape(steps, nbs, N_HEADS * KV_LORA), xs, ms, row(norm_mix_post[i]),
                          w['wo'], w['bduv'])
            outs['kv_s'].append(ckv_sb)
            outs['kr_s'].append(kr_sb)
        else:
            wg = pool_w[j].astype(BF16)
            xp, st_p = _pool_prompt(xp, mods_prompt(i, 0), row(norm_mix_pre[i]), row(norm_mix_post[i]), wg,
                                    row(pool_scale[j]), seq=seq)
            outs['pool_p'].append(st_p[:, POOL_HALO - POOL_BUF:])
            xs, st_s = _pool_sample(xs, tmajor(state_pool[j]), mods_sample(i, 0), row(norm_mix_pre[i]),
                                    row(norm_mix_post[i]), wg, row(pool_scale[j]), past=past)
            outs['pool_s'].append(tmajor(st_s))

        wup = ffn_w_up[i].astype(BF16)
        wdn = ffn_w_down[i].astype(BF16)
        xp, cst_p = _ffn_prompt(xp, mods_prompt(i, 3), row(norm_ffn_pre[i]), row(norm_ffn_post[i]), wup,
                                ffn_conv_w[i], row(ffn_conv_b[i]), wdn, seq=seq)
        outs['conv_p'].append(cst_p[:, CONV_HALO - CONV_BUF:])
        xs, cst_s = _ffn_sample(xs, tmajor(state_conv[i]), mods_sample(i, 3), row(norm_ffn_pre[i]),
                                row(norm_ffn_post[i]), wup, ffn_conv_w[i], row(ffn_conv_b[i]), wdn)
        outs['conv_s'].append(tmajor(cst_s))

    st = lambda key: jnp.stack(outs[key])
    return (xp.reshape(nbp, seq, d), tmajor(xs), st('kv_p'), st('kr_p'), st('kv_s'), st('kr_s'),
            st('pool_p'), st('pool_s'), st('conv_p'), st('conv_s'))
```

```python
import functools

import jax
import jax.numpy as jnp
from jax import lax
from jax.experimental import pallas as pl
from jax.experimental.pallas import tpu as pltpu

F32 = jnp.float32
BF16 = jnp.bfloat16

D_MODEL = 1024
N_HEADS = 16
QK_NOPE = 64
QK_ROPE = 32
V_HEAD = 64
Q_LORA = 512
KV_LORA = 256
ROPE_THETA = 10000.0
ATTN_SCALE = (QK_NOPE + QK_ROPE) ** -0.5
PAGE_SIZE = 128
POOL_WINDOWS = (2, 4, 8, 16)
GROUP_W = D_MODEL // len(POOL_WINDOWS)
POOL_BUF = 15
D_FF = 2816
CONV_W = 3
CONV_BUF = CONV_W - 1
EPS = 1e-6

HEAD_PAD = 128
POOL_HALO = 16
CONV_HALO = 16
STATE_ROWS = 8
FF_CHUNK = 256
NEW_KEY_ROWS = 16
PROMPT_TILE = 512
FFN_TILE = 512
SAMPLE_TILE = 64
FLASH_TILE = 256
FLASH_HEADS = 4
VT_ROWS = 80
LOG2E = 1.4426950408889634
VMEM_LIMIT = 56 * 1024 * 1024


def _cparams(sem):
    return pltpu.CompilerParams(dimension_semantics=sem, vmem_limit_bytes=VMEM_LIMIT)


def _dot(a, b):
    return jnp.dot(a, b, preferred_element_type=F32)


def _dot_t(a, b):
    return lax.dot_general(a, b, (((1,), (1,)), ((), ())), preferred_element_type=F32)


def _rms(x, g):
    return x * lax.rsqrt(jnp.mean(x * x, axis=-1, keepdims=True) + EPS) * g


def _modulate(x, g, shift, scale):
    return _rms(x, g) * (1.0 + scale) + shift


def _silu(x):
    return x * (1.0 / (1.0 + jnp.exp(-x)))


def _resident(shape):
    zeros = (0,) * len(shape)
    return pl.BlockSpec(shape, lambda *_: zeros, pipeline_mode=pl.Buffered(1))


def _ld(ref):
    v = ref[...]
    return v if v.ndim == 2 else v.reshape(-1, v.shape[-1])


def _st(ref, val):
    ref[...] = val.reshape(ref.shape).astype(ref.dtype)


def _mod(mod_ref, k, rows):
    m = mod_ref[k]
    if m.shape[0] in (1, rows):
        return m
    return jnp.concatenate([m] * (rows // m.shape[0]), axis=0)


class _Rows:
    def __init__(self, x, seq=None, tile=PROMPT_TILE):
        self.sample = x.ndim == 3
        if self.sample:
            self.steps, self.nb = x.shape[0], x.shape[1]
            self.nbt = min(SAMPLE_TILE, self.nb)
            self.grid = self.nb // self.nbt
            self.tile_rows = self.steps * self.nbt
        else:
            self.seq = seq
            self.tm = min(tile, seq)
            self.grid = x.shape[0] // self.tm
            self.tps = seq // self.tm
            self.tile_rows = self.tm

    def spec(self, w):
        if self.sample:
            return pl.BlockSpec((self.steps, self.nbt, w), lambda i: (0, i, 0))
        return pl.BlockSpec((self.tm, w), lambda i: (i, 0))

    def shape(self, x, w, dtype):
        return jax.ShapeDtypeStruct(x.shape[:-1] + (w,), dtype)

    def mod_spec(self):
        if self.sample:
            return pl.BlockSpec((3, self.nbt, D_MODEL), lambda i: (0, i, 0))
        tps = self.tps
        return pl.BlockSpec((3, None, 1, D_MODEL), lambda i: (0, i // tps, 0, 0))

    def table_spec(self, w):
        if self.sample:
            return _resident((self.tile_rows, w))
        tps = self.tps
        return pl.BlockSpec((self.tm, w), lambda i: (i % tps, 0))


def _adaln_kernel(c_ref, w_ref, b_ref, o_ref):
    a = _silu(c_ref[...]).astype(BF16)
    o_ref[...] = _dot(a, w_ref[...].astype(BF16)) + b_ref[...]


def _adaln(c_all, w, b):
    depth, d, n = w.shape
    rows = c_all.shape[0]
    return pl.pallas_call(
        _adaln_kernel,
        grid=(depth, n // d),
        in_specs=[
            pl.BlockSpec((rows, d), lambda i, j: (0, 0)),
            pl.BlockSpec((None, d, d), lambda i, j: (i, 0, j)),
            pl.BlockSpec((None, 1, d), lambda i, j: (i, 0, j)),
        ],
        out_specs=pl.BlockSpec((None, None, rows, d), lambda i, j: (i, j, 0, 0)),
        out_shape=jax.ShapeDtypeStruct((depth, n // d, rows, d), F32),
        compiler_params=_cparams(("parallel", "parallel")),
        name="adaln",
    )(c_all, w, b.reshape(depth, 1, n))


def _proj_common(x_ref, mod_ref, npre_ref, cosk, sink, wdq_ref, qn_ref, wkv_ref, kvn_ref, ckv_ref, kr_ref):
    x = _ld(x_ref)
    rows = x.shape[0]
    h = _modulate(x, npre_ref[...], _mod(mod_ref, 0, rows), _mod(mod_ref, 1, rows)).astype(BF16)
    cq = _rms(_dot(h, wdq_ref[...]), qn_ref[...]).astype(BF16)
    kv = _dot(h, wkv_ref[...])
    ckv = _rms(kv[:, :KV_LORA], kvn_ref[...])
    krp = kv[:, KV_LORA:KV_LORA + HEAD_PAD] * cosk + kv[:, KV_LORA + HEAD_PAD:] * sink
    _st(ckv_ref, ckv)
    _st(kr_ref, krp[:, :QK_ROPE])
    return cq, ckv, krp


def _proj_prompt_kernel(x_ref, mod_ref, npre_ref, cos_ref, sin_ref, wdq_ref, qn_ref, wqa_ref, wqb_ref,
                        wkv_ref, kvn_ref, wk_ref, wvt_ref, vbias_ref,
                        q_ref, k_ref, vt_ref, ckv_ref, kr_ref):
    cos = cos_ref[...]
    sin = sin_ref[...]
    cq, ckv, krp = _proj_common(x_ref, mod_ref, npre_ref, cos, sin, wdq_ref, qn_ref, wkv_ref, kvn_ref,
                                ckv_ref, kr_ref)
    qa = _dot(cq, wqa_ref[...])
    qb = _dot(cq, wqb_ref[...])
    for hd in range(N_HEADS):
        sl = slice(hd * HEAD_PAD, (hd + 1) * HEAD_PAD)
        q_ref[:, sl] = (qa[:, sl] * cos + qb[:, sl] * sin).astype(q_ref.dtype)
    ckv_b = ckv.astype(BF16)
    kcat = jnp.concatenate([ckv_b, krp.astype(BF16)], axis=-1)
    k_ref[...] = _dot(kcat, wk_ref[...]).astype(k_ref.dtype)
    vt = (_dot_t(wvt_ref[...], ckv_b) + vbias_ref[...]).astype(vt_ref.dtype)
    tk = vt_ref.shape[-1]
    for c in range(vt_ref.shape[0]):
        vt_ref[c] = vt[:, c * tk:(c + 1) * tk]


def _proj_prompt(x2d, mod, npre, cos_t, sin_t, wdq, qn, wqa, wqb, wkv, kvn, wk, wvt, vbias, *, seq):
    r = _Rows(x2d, seq)
    hq = N_HEADS * HEAD_PAD
    tk = min(FLASH_TILE, seq)
    kpt = r.tm // tk
    nbp = x2d.shape[0] // seq
    vt_rows = N_HEADS * VT_ROWS
    return pl.pallas_call(
        _proj_prompt_kernel,
        grid=(r.grid,),
        in_specs=[
            r.spec(D_MODEL), r.mod_spec(), _resident((1, D_MODEL)),
            r.table_spec(HEAD_PAD), r.table_spec(HEAD_PAD),
            _resident(wdq.shape), _resident(qn.shape), _resident(wqa.shape), _resident(wqb.shape),
            _resident(wkv.shape), _resident(kvn.shape), _resident(wk.shape), _resident(wvt.shape),
            _resident(vbias.shape),
        ],
        out_specs=[r.spec(hq), r.spec(hq),
                   pl.BlockSpec((None, kpt, vt_rows, tk), lambda i: (i // r.tps, i % r.tps, 0, 0)),
                   r.spec(KV_LORA), r.spec(QK_ROPE)],
        out_shape=[r.shape(x2d, hq, BF16), r.shape(x2d, hq, BF16),
                   jax.ShapeDtypeStruct((nbp, seq // tk, vt_rows, tk), BF16),
                   r.shape(x2d, KV_LORA, F32), r.shape(x2d, QK_ROPE, F32)],
        compiler_params=_cparams(("parallel",)),
        name="mla_proj_prompt",
    )(x2d, mod, npre, cos_t, sin_t, wdq, qn, wqa, wqb, wkv, kvn, wk, wvt, vbias)


def _proj_sample_kernel(x_ref, mod_ref, npre_ref, cosq_ref, sinq_ref, cosk_ref, sink_ref,
                        wdq_ref, qn_ref, wqn_ref, wqa_ref, wqb_ref, wkv_ref, kvn_ref, bduk_ref,
                        ql_ref, qr_ref, ckv_ref, kr_ref):
    cq, _, _ = _proj_common(x_ref, mod_ref, npre_ref, cosk_ref[...], sink_ref[...], wdq_ref, qn_ref, wkv_ref,
                            kvn_ref, ckv_ref, kr_ref)
    q_nope = _dot(cq, wqn_ref[...]).astype(BF16)
    _st(ql_ref, _dot(q_nope, bduk_ref[...]))
    _st(qr_ref, _dot(cq, wqa_ref[...]) * cosq_ref[...] + _dot(cq, wqb_ref[...]) * sinq_ref[...])


def _proj_sample(x3d, mod, npre, cosq, sinq, cosk, sink, wdq, qn, wqn, wqa, wqb, wkv, kvn, bduk):
    r = _Rows(x3d)
    return pl.pallas_call(
        _proj_sample_kernel,
        grid=(r.grid,),
        in_specs=[
            r.spec(D_MODEL), r.mod_spec(), _resident((1, D_MODEL)),
            r.table_spec(N_HEADS * QK_ROPE), r.table_spec(N_HEADS * QK_ROPE),
            r.table_spec(HEAD_PAD), r.table_spec(HEAD_PAD),
            _resident(wdq.shape), _resident(qn.shape), _resident(wqn.shape), _resident(wqa.shape),
            _resident(wqb.shape), _resident(wkv.shape), _resident(kvn.shape), _resident(bduk.shape),
        ],
        out_specs=[r.spec(N_HEADS * KV_LORA), r.spec(N_HEADS * QK_ROPE), r.spec(KV_LORA), r.spec(QK_ROPE)],
        out_shape=[r.shape(x3d, N_HEADS * KV_LORA, BF16), r.shape(x3d, N_HEADS * QK_ROPE, BF16),
                   r.shape(x3d, KV_LORA, F32), r.shape(x3d, QK_ROPE, F32)],
        compiler_params=_cparams(("parallel",)),
        name="mla_proj_sample",
    )(x3d, mod, npre, cosq, sinq, cosk, sink, wdq, qn, wqn, wqa, wqb, wkv, kvn, bduk)


def _flash_kernel(q_ref, k_ref, vt_ref, o_ref, sa_ref, sb_ref, *, tq, heads):
    qi = pl.program_id(2)
    krow = lax.broadcasted_iota(jnp.int32, (tq, tq), 0)
    qcol = lax.broadcasted_iota(jnp.int32, (tq, tq), 1)
    qs = [q_ref[:, h * HEAD_PAD:(h + 1) * HEAD_PAD] for h in range(heads)]

    def scores(ki, s_ref):
        k0 = pl.multiple_of(ki * tq, tq)
        for h in range(heads):
            s_ref[h] = _dot_t(k_ref[pl.ds(k0, tq), h * HEAD_PAD:(h + 1) * HEAD_PAD], qs[h])

    def consume(ki, s_ref, carry, masked):
        pts, ms, alphas = [], [], []
        for h in range(heads):
            st = s_ref[h]
            if masked:
                st = jnp.where(krow + (ki - qi) * tq <= qcol, st, -jnp.inf)
            m_new = jnp.maximum(carry[h][0], jnp.max(st, axis=0, keepdims=True))
            alphas.append(jnp.exp2(carry[h][0] - m_new))
            pts.append(jnp.exp2(st - m_new).astype(BF16))
            ms.append(m_new)
        new = []
        for h in range(heads):
            pv = _dot(vt_ref[ki, h * VT_ROWS:(h + 1) * VT_ROWS, :], pts[h])
            new.append((ms[h], alphas[h] * carry[h][1] + pv))
        return tuple(new)

    def pair(j, carry, masked):
        scores(2 * j + 1, sb_ref)
        carry = consume(2 * j, sa_ref, carry, masked)
        if not masked:
            scores(2 * j + 2, sa_ref)
        return consume(2 * j + 1, sb_ref, carry, masked)

    scores(0, sa_ref)
    init = tuple((jnp.full((1, tq), -jnp.inf, F32), jnp.zeros((VT_ROWS, tq), F32)) for _ in range(heads))
    carry = lax.fori_loop(0, qi // 2, functools.partial(pair, masked=False), init)
    carry = pair(qi // 2, carry, True)
    for h in range(heads):
        acc = carry[h][1]
        o_ref[h * V_HEAD:(h + 1) * V_HEAD, :] = (acc[:V_HEAD] / acc[V_HEAD:V_HEAD + 1]).astype(o_ref.dtype)


def _flash(q, k, vt):
    nb, seq, _ = q.shape
    tq = vt.shape[-1]
    heads = FLASH_HEADS
    assert (seq // tq) % 2 == 0 and N_HEADS % heads == 0
    return pl.pallas_call(
        functools.partial(_flash_kernel, tq=tq, heads=heads),
        grid=(nb, N_HEADS // heads, seq // tq),
        in_specs=[
            pl.BlockSpec((None, tq, heads * HEAD_PAD), lambda b, g, i: (b, i, g)),
            pl.BlockSpec((None, seq, heads * HEAD_PAD), lambda b, g, i: (b, 0, g)),
            pl.BlockSpec((None, seq // tq, heads * VT_ROWS, tq), lambda b, g, i: (b, 0, g, 0)),
        ],
        out_specs=pl.BlockSpec((None, heads * V_HEAD, tq), lambda b, g, i: (b, g, i)),
        out_shape=jax.ShapeDtypeStruct((nb, N_HEADS * V_HEAD, seq), BF16),
        scratch_shapes=[pltpu.VMEM((heads, tq, tq), F32), pltpu.VMEM((heads, tq, tq), F32)],
        compiler_params=_cparams(("parallel", "parallel", "arbitrary")),
        name="flash_prompt",
    )(q, k, vt)


def _sattn_kernel(pt_ref, ql_ref, qr_ref, cn_ref, rn_ref, ckv_hbm, krt_hbm, o_ref, kvbuf, krbuf, sem,
                  s_a, s_b, kv_a, kv_b, *, layer, n_pages, chunk_pages):
    b = pl.program_id(0)
    nb = pl.num_programs(0)
    slot = b % 2

    def page_copies(bb, sl, p):
        page = pt_ref[bb * n_pages + p]
        return (pltpu.make_async_copy(ckv_hbm.at[layer, page], kvbuf.at[sl, p], sem.at[0, sl]),
                pltpu.make_async_copy(krt_hbm.at[layer, page], krbuf.at[sl, p], sem.at[1, sl]))

    def start_fetch(bb, sl):
        def body(p, c):
            for cp in page_copies(bb, sl, p):
                cp.start()
            return c
        lax.fori_loop(0, n_pages, body, 0, unroll=8)

    def wait_fetch(bb, sl):
        def body(p, c):
            for cp in page_copies(bb, sl, p):
                cp.wait()
            return c
        lax.fori_loop(0, n_pages, body, 0, unroll=8)

    @pl.when(b == 0)
    def _():
        start_fetch(0, 0)

    @pl.when(b + 1 < nb)
    def _():
        start_fetch(b + 1, 1 - slot)

    wait_fetch(b, slot)

    ql = ql_ref[...].reshape(-1, KV_LORA)
    qr = qr_ref[...].reshape(-1, QK_ROPE)
    rows = ql.shape[0]
    ck = chunk_pages * PAGE_SIZE

    s_bufs = (s_a, s_b)
    kv_bufs = (kv_a, kv_b)

    def scores(c):
        p0 = c * chunk_pages
        kv = kvbuf[slot, p0:p0 + chunk_pages].reshape(ck, KV_LORA).astype(BF16)
        kv_bufs[c % 2][...] = kv
        krt = jnp.concatenate([krbuf[slot, p0 + i] for i in range(chunk_pages)], axis=1).astype(BF16)
        s_bufs[c % 2][...] = _dot_t(ql, kv) + _dot(qr, krt)

    def softmax_step(s, values, carry):
        m, l, acc = carry
        m_new = jnp.maximum(m, jnp.max(s, axis=-1, keepdims=True))
        alpha = jnp.exp(m - m_new)
        p = jnp.exp(s - m_new)
        l = alpha * l + jnp.sum(p, axis=-1, keepdims=True)
        return m_new, l, alpha * acc + _dot(p.astype(BF16), values)

    carry = (jnp.full((rows, 1), -jnp.inf, F32), jnp.zeros((rows, 1), F32), jnp.zeros((rows, KV_LORA), F32))
    n_chunks = n_pages // chunk_pages
    scores(0)
    for c in range(n_chunks):
        if c + 1 < n_chunks:
            scores(c + 1)
        carry = softmax_step(s_bufs[c % 2][...], kv_bufs[c % 2][...], carry)

    cn = cn_ref[...].astype(BF16)
    rn = rn_ref[...].astype(BF16)
    s = _dot_t(ql, cn) + _dot_t(qr, rn)
    qstep = lax.broadcasted_iota(jnp.int32, s.shape, 0) // N_HEADS
    kstep = lax.broadcasted_iota(jnp.int32, s.shape, 1)
    s = jnp.where(kstep <= qstep, s, -jnp.inf)
    _, l, acc = softmax_step(s, cn, carry)
    o_ref[...] = (acc / l).astype(o_ref.dtype).reshape(o_ref.shape)


def _sattn(page_table, ql, qr, cn, rn, cache_kv, cache_krt, *, layer, chunk_pages):
    steps, nb = ql.shape[0], ql.shape[1]
    rows = steps * N_HEADS
    n_pages = page_table.shape[1]
    n_new = cn.shape[1]
    ck = chunk_pages * PAGE_SIZE
    qspec = lambda w: pl.BlockSpec((steps, None, N_HEADS, w), lambda b, pt: (0, b, 0, 0))
    grid_spec = pltpu.PrefetchScalarGridSpec(
        num_scalar_prefetch=1,
        grid=(nb,),
        in_specs=[
            qspec(KV_LORA),
            qspec(QK_ROPE),
            pl.BlockSpec((None, n_new, KV_LORA), lambda b, pt: (b, 0, 0)),
            pl.BlockSpec((None, n_new, QK_ROPE), lambda b, pt: (b, 0, 0)),
            pl.BlockSpec(memory_space=pl.ANY),
            pl.BlockSpec(memory_space=pl.ANY),
        ],
        out_specs=qspec(KV_LORA),
        scratch_shapes=[
            pltpu.VMEM((2, n_pages, PAGE_SIZE, KV_LORA), F32),
            pltpu.VMEM((2, n_pages, QK_ROPE, PAGE_SIZE), F32),
            pltpu.SemaphoreType.DMA((2, 2)),
            pltpu.VMEM((rows, ck), F32), pltpu.VMEM((rows, ck), F32),
            pltpu.VMEM((ck, KV_LORA), BF16), pltpu.VMEM((ck, KV_LORA), BF16),
        ],
    )
    return pl.pallas_call(
        functools.partial(_sattn_kernel, layer=layer, n_pages=n_pages, chunk_pages=chunk_pages),
        grid_spec=grid_spec,
        out_shape=jax.ShapeDtypeStruct((steps, nb, N_HEADS, KV_LORA), BF16),
        compiler_params=_cparams(("arbitrary",)),
        name="sample_attend",
    )(page_table.reshape(-1), ql, qr, cn, rn, cache_kv, cache_krt)


def _mla_out_kernel(*refs, absorbed):
    if absorbed:
        o_ref, x_ref, mod_ref, npost_ref, bduv_ref, wo_ref, y_ref = refs
        o = _dot(_ld(o_ref), bduv_ref[...]).astype(BF16)
        f = _dot(o, wo_ref[...])
    else:
        o_ref, x_ref, mod_ref, npost_ref, wo_ref, y_ref = refs
        f = lax.dot_general(o_ref[...], wo_ref[...], (((0,), (0,)), ((), ())), preferred_element_type=F32)
    x = _ld(x_ref)
    _st(y_ref, x + _mod(mod_ref, 2, x.shape[0]) * _rms(f, npost_ref[...]))


def _mla_out(o, x, mod, npost, wo, bduv=None, *, seq=None):
    r = _Rows(x, seq)
    if bduv is not None:
        o_spec = r.spec(o.shape[-1])
    else:
        o_spec = pl.BlockSpec((None, o.shape[1], r.tm), lambda i: (i // r.tps, 0, i % r.tps))
    in_specs = [o_spec, r.spec(D_MODEL), r.mod_spec(), _resident((1, D_MODEL))]
    args = [o, x, mod, npost]
    if bduv is not None:
        in_specs.append(_resident(bduv.shape))
        args.append(bduv)
    in_specs.append(_resident(wo.shape))
    args.append(wo)
    return pl.pallas_call(
        functools.partial(_mla_out_kernel, absorbed=bduv is not None),
        grid=(r.grid,),
        in_specs=in_specs,
        out_specs=r.spec(D_MODEL),
        out_shape=r.shape(x, D_MODEL, F32),
        compiler_params=_cparams(("parallel",)),
        name="mla_out_sample" if bduv is not None else "mla_out_prompt",
    )(*args)


def _pool_finish(window_means, h, x, mod_ref, npost_ref, wg_ref, ls_ref, y_ref):
    outs = []
    for g in range(len(POOL_WINDOWS)):
        cols = slice(g * GROUP_W, (g + 1) * GROUP_W)
        pooled = (window_means[g] - h[:, cols]).astype(BF16)
        outs.append(_dot(pooled, wg_ref[g]))
    o = jnp.concatenate(outs, axis=-1) * ls_ref[...]
    _st(y_ref, x + _mod(mod_ref, 2, x.shape[0]) * _rms(o, npost_ref[...]))


def _pool_prompt_kernel(x_ref, xh_ref, mod_ref, npre_ref, npost_ref, wg_ref, ls_ref, y_ref, st_ref, hext_ref,
                        *, tm, tps):
    t_in_seq = pl.program_id(0) % tps
    keep = jnp.where(t_in_seq == 0, 0.0, 1.0)
    x = x_ref[...]
    h = _modulate(x, npre_ref[...], mod_ref[0], mod_ref[1])
    hh = _modulate(xh_ref[...], npre_ref[...], mod_ref[0], mod_ref[1]) * keep
    hext_ref[0:POOL_HALO, :] = hh
    hext_ref[POOL_HALO:, :] = h
    st_ref[...] = h[tm - POOL_HALO:, :]
    pos = t_in_seq * tm + lax.broadcasted_iota(jnp.int32, (tm, 1), 0)
    means = []
    for g, w in enumerate(POOL_WINDOWS):
        cols = slice(g * GROUP_W, (g + 1) * GROUP_W)
        s = h[:, cols]
        for j in range(1, w):
            s = s + hext_ref[POOL_HALO - j:POOL_HALO - j + tm, cols]
        means.append(s / jnp.minimum(pos + 1, w).astype(F32))
    _pool_finish(means, h, x, mod_ref, npost_ref, wg_ref, ls_ref, y_ref)


def _pool_prompt(x2d, mod, npre, npost, wg, ls, *, seq):
    r = _Rows(x2d, seq)
    hb = r.tm // POOL_HALO
    return pl.pallas_call(
        functools.partial(_pool_prompt_kernel, tm=r.tm, tps=r.tps),
        grid=(r.grid,),
        in_specs=[
            r.spec(D_MODEL),
            pl.BlockSpec((POOL_HALO, D_MODEL), lambda i: (jnp.maximum(i * hb - 1, 0), 0)),
            r.mod_spec(),
            _resident((1, D_MODEL)), _resident((1, D_MODEL)), _resident(wg.shape), _resident((1, D_MODEL)),
        ],
        out_specs=[r.spec(D_MODEL), pl.BlockSpec((None, POOL_HALO, D_MODEL), lambda i: (i // r.tps, 0, 0))],
        out_shape=[r.shape(x2d, D_MODEL, F32),
                   jax.ShapeDtypeStruct((x2d.shape[0] // seq, POOL_HALO, D_MODEL), F32)],
        scratch_shapes=[pltpu.VMEM((r.tm + POOL_HALO, D_MODEL), F32)],
        compiler_params=_cparams(("arbitrary",)),
        name="pool_prompt",
    )(x2d, x2d, mod, npre, npost, wg, ls)


def _pool_sample_kernel(x_ref, st_ref, mod_ref, npre_ref, npost_ref, wg_ref, ls_ref, y_ref, so_ref, *, past):
    steps, nbt, _ = x_ref.shape
    x = _ld(x_ref)
    h = _modulate(x, npre_ref[...], _mod(mod_ref, 0, x.shape[0]), _mod(mod_ref, 1, x.shape[0]))

    def slab(k, cols):
        if k < POOL_BUF:
            return st_ref[k, :, cols]
        return h[(k - POOL_BUF) * nbt:(k - POOL_BUF + 1) * nbt, cols]

    for k in range(POOL_BUF):
        so_ref[k] = slab(k + steps, slice(None))
    means = []
    for g, w in enumerate(POOL_WINDOWS):
        cols = slice(g * GROUP_W, (g + 1) * GROUP_W)
        parts = []
        for s_ in range(steps):
            acc = slab(POOL_BUF + s_, cols)
            for j in range(1, w):
                acc = acc + slab(POOL_BUF + s_ - j, cols)
            parts.append(acc / float(min(past + s_ + 1, w)))
        means.append(jnp.concatenate(parts, axis=0))
    _pool_finish(means, h, x, mod_ref, npost_ref, wg_ref, ls_ref, y_ref)


def _pool_sample(x3d, st3d, mod, npre, npost, wg, ls, *, past):
    r = _Rows(x3d)
    st_spec = pl.BlockSpec((POOL_BUF, r.nbt, D_MODEL), lambda i: (0, i, 0))
    return pl.pallas_call(
        functools.partial(_pool_sample_kernel, past=past),
        grid=(r.grid,),
        in_specs=[r.spec(D_MODEL), st_spec, r.mod_spec(),
                  _resident((1, D_MODEL)), _resident((1, D_MODEL)), _resident(wg.shape), _resident((1, D_MODEL))],
        out_specs=[r.spec(D_MODEL), st_spec],
        out_shape=[r.shape(x3d, D_MODEL, F32), jax.ShapeDtypeStruct(st3d.shape, F32)],
        compiler_params=_cparams(("parallel",)),
        name="pool_sample",
    )(x3d, st3d, mod, npre, npost, wg, ls)


def _ffn_chunks(h, bufs, put_fn, conv_fn, cw_ref, cb_ref, wup_ref, wdn_ref):
    n = D_FF // FF_CHUNK
    cols = lambda c, off: slice(off + c * FF_CHUNK, off + (c + 1) * FF_CHUNK)

    def up(c):
        for k, off in enumerate((0, D_FF)):
            put_fn(bufs[c % 2], k, _dot(h, wup_ref[:, cols(c, off)]))

    up(0)
    acc = None
    for c in range(n):
        halves = []
        for k, off in enumerate((0, D_FF)):
            u, r1, r2 = conv_fn(bufs[c % 2], k, cols(c, off))
            cw = cw_ref[:, cols(c, off)]
            halves.append(cb_ref[:, cols(c, off)] + r2 * cw[0:1] + r1 * cw[1:2] + u * cw[2:3])
        if c + 1 < n:
            up(c + 1)
        gate = (_silu(halves[0]) * halves[1]).astype(BF16)
        part = _dot(gate, wdn_ref[c * FF_CHUNK:(c + 1) * FF_CHUNK, :])
        acc = part if acc is None else acc + part
    return acc


def _ffn_prompt_kernel(x_ref, xh_ref, mod_ref, npre_ref, npost_ref, wup_ref, cw_ref, cb_ref, wdn_ref,
                       y_ref, st_ref, buf_a, buf_b, *, tm, tps):
    keep = jnp.where(pl.program_id(0) % tps == 0, 0.0, 1.0)
    x = x_ref[...]
    xe = jnp.concatenate([xh_ref[...], x], axis=0)
    he = _modulate(xe, npre_ref[...], mod_ref[0], mod_ref[1]).astype(BF16)
    hl = CONV_HALO

    def put_fn(buf, k, ue):
        buf[k, :hl] = ue[:hl] * keep
        buf[k, hl:] = ue[hl:]

    def conv_fn(buf, k, cols):
        st_ref[:, cols] = buf[k, hl + tm - STATE_ROWS:hl + tm]
        return buf[k, hl:hl + tm], buf[k, hl - 1:hl - 1 + tm], buf[k, hl - 2:hl - 2 + tm]

    f = _ffn_chunks(he, (buf_a, buf_b), put_fn, conv_fn, cw_ref, cb_ref, wup_ref, wdn_ref)
    y_ref[...] = x + mod_ref[2] * _rms(f, npost_ref[...])


def _ffn_prompt(x2d, mod, npre, npost, wup, cw, cb, wdn, *, layer, seq):
    r = _Rows(x2d, seq, FFN_TILE)
    hb = r.tm // CONV_HALO
    lyr = lambda a: pl.BlockSpec((None,) + a.shape[1:], lambda i: (layer,) + (0,) * (a.ndim - 1),
                                 pipeline_mode=pl.Buffered(1))
    return pl.pallas_call(
        functools.partial(_ffn_prompt_kernel, tm=r.tm, tps=r.tps),
        grid=(r.grid,),
        in_specs=[
            r.spec(D_MODEL),
            pl.BlockSpec((CONV_HALO, D_MODEL), lambda i: (jnp.maximum(i * hb - 1, 0), 0)),
            r.mod_spec(),
            _resident((1, D_MODEL)), _resident((1, D_MODEL)),
            lyr(wup), _resident(cw.shape), _resident(cb.shape), lyr(wdn),
        ],
        out_specs=[r.spec(D_MODEL), pl.BlockSpec((None, STATE_ROWS, 2 * D_FF), lambda i: (i // r.tps, 0, 0))],
        out_shape=[r.shape(x2d, D_MODEL, F32),
                   jax.ShapeDtypeStruct((x2d.shape[0] // seq, STATE_ROWS, 2 * D_FF), F32)],
        scratch_shapes=[pltpu.VMEM((2, r.tm + CONV_HALO, FF_CHUNK), F32)] * 2,
        compiler_params=_cparams(("arbitrary",)),
        name="ffn_prompt",
    )(x2d, x2d, mod, npre, npost, wup, cw, cb, wdn)


def _ffn_sample_kernel(x_ref, st_ref, mod_ref, npre_ref, npost_ref, wup_ref, cw_ref, cb_ref, wdn_ref,
                       y_ref, so_ref, buf_a, buf_b):
    steps, nbt, _ = x_ref.shape
    x = _ld(x_ref)
    rows = x.shape[0]
    h = _modulate(x, npre_ref[...], _mod(mod_ref, 0, rows), _mod(mod_ref, 1, rows)).astype(BF16)

    def put_fn(buf, k, u):
        buf[k] = u

    def conv_fn(buf, k, cols):
        s0 = st_ref[0, :, cols]
        s1 = st_ref[1, :, cols]
        so_ref[0, :, cols] = buf[k, rows - 2 * nbt:rows - nbt]
        so_ref[1, :, cols] = buf[k, rows - nbt:rows]
        return (buf[k], jnp.concatenate([s1, buf[k, :rows - nbt]], axis=0),
                jnp.concatenate([s0, s1, buf[k, :rows - 2 * nbt]], axis=0))

    f = _ffn_chunks(h, (buf_a, buf_b), put_fn, conv_fn, cw_ref, cb_ref, wup_ref, wdn_ref)
    _st(y_ref, x + _mod(mod_ref, 2, rows) * _rms(f, npost_ref[...]))


def _ffn_sample(x3d, st3d, mod, npre, npost, wup, cw, cb, wdn, *, layer):
    r = _Rows(x3d)
    st_spec = pl.BlockSpec((CONV_BUF, r.nbt, 2 * D_FF), lambda i: (0, i, 0))
    lyr = lambda a: pl.BlockSpec((None,) + a.shape[1:], lambda i: (layer,) + (0,) * (a.ndim - 1),
                                 pipeline_mode=pl.Buffered(1))
    return pl.pallas_call(
        _ffn_sample_kernel,
        grid=(r.grid,),
        in_specs=[r.spec(D_MODEL), st_spec, r.mod_spec(),
                  _resident((1, D_MODEL)), _resident((1, D_MODEL)),
                  lyr(wup), _resident(cw.shape), _resident(cb.shape), lyr(wdn)],
        out_specs=[r.spec(D_MODEL), st_spec],
        out_shape=[r.shape(x3d, D_MODEL, F32), jax.ShapeDtypeStruct(st3d.shape, F32)],
        scratch_shapes=[pltpu.VMEM((2, r.tile_rows, FF_CHUNK), F32)] * 2,
        compiler_params=_cparams(("parallel",)),
        name="ffn_sample",
    )(x3d, st3d, mod, npre, npost, wup, cw, cb, wdn)


def _rope_angles(pos):
    half = QK_ROPE // 2
    inv = ROPE_THETA ** (-jnp.arange(half, dtype=F32) / half)
    ang = pos.astype(F32)[:, None] * inv[None, :]
    return jnp.cos(ang), jnp.sin(ang)


def _rope_tables(pos, width):
    cos, sin = _rope_angles(pos)
    n = pos.shape[0]
    cos_t = jnp.concatenate([cos, cos, jnp.ones((n, width - QK_ROPE), F32)], axis=-1)
    sin_t = jnp.concatenate([-sin, sin, jnp.zeros((n, width - QK_ROPE), F32)], axis=-1)
    return cos_t, sin_t


def _swap_halves(w):
    half = QK_ROPE // 2
    return jnp.concatenate([w[..., half:], w[..., :half]], axis=-1)


def _mla_weights(w_dq, w_uq, w_dkv, w_uk, w_uv, w_o):
    wq3 = w_uq.reshape(Q_LORA, N_HEADS, QK_NOPE + QK_ROPE) * ATTN_SCALE
    nope, ropew = wq3[..., :QK_NOPE], wq3[..., QK_NOPE:]
    ropesw = _swap_halves(ropew)
    zpad = jnp.zeros((Q_LORA, N_HEADS, HEAD_PAD - QK_NOPE - QK_ROPE), F32)
    wqa_p = jnp.concatenate([ropew, nope, zpad], axis=-1).reshape(Q_LORA, N_HEADS * HEAD_PAD)
    wqb_p = jnp.concatenate([ropesw, jnp.zeros((Q_LORA, N_HEADS, HEAD_PAD - QK_ROPE), F32)], axis=-1)
    wqb_p = wqb_p.reshape(Q_LORA, N_HEADS * HEAD_PAD)
    kr_w = w_dkv[:, KV_LORA:]
    lane_pad = jnp.zeros((D_MODEL, HEAD_PAD - QK_ROPE), F32)
    wkv = jnp.concatenate([w_dkv[:, :KV_LORA], kr_w, lane_pad, _swap_halves(kr_w), lane_pad], axis=-1)
    eye_r = jnp.eye(QK_ROPE, dtype=F32)
    top = jnp.concatenate([jnp.zeros((KV_LORA, N_HEADS, QK_ROPE), F32), w_uk,
                           jnp.zeros((KV_LORA, N_HEADS, HEAD_PAD - QK_NOPE - QK_ROPE), F32)], axis=-1)
    mid = jnp.concatenate([jnp.broadcast_to(eye_r[:, None, :], (QK_ROPE, N_HEADS, QK_ROPE)),
                           jnp.zeros((QK_ROPE, N_HEADS, HEAD_PAD - QK_ROPE), F32)], axis=-1)
    bot = jnp.zeros((HEAD_PAD - QK_ROPE, N_HEADS, HEAD_PAD), F32)
    wk = jnp.concatenate([top, mid, bot], axis=0).reshape(KV_LORA + HEAD_PAD, N_HEADS * HEAD_PAD)
    wvt = jnp.pad(jnp.transpose(w_uv, (1, 2, 0)), ((0, 0), (0, VT_ROWS - V_HEAD), (0, 0)))
    wvt = wvt.reshape(N_HEADS * VT_ROWS, KV_LORA)
    vbias = jnp.tile((jnp.arange(VT_ROWS) == V_HEAD).astype(F32), N_HEADS).reshape(N_HEADS * VT_ROWS, 1)
    eye_h = jnp.eye(N_HEADS, dtype=F32)
    bduk = jnp.einsum('hg,khn->hngk', eye_h, w_uk).reshape(N_HEADS * QK_NOPE, N_HEADS * KV_LORA)
    bduv = jnp.einsum('hg,khv->hkgv', eye_h, w_uv).reshape(N_HEADS * KV_LORA, N_HEADS * V_HEAD)
    b = lambda a: a.astype(BF16)
    return dict(
        wdq=b(w_dq), wqa_p=b(wqa_p * LOG2E), wqb_p=b(wqb_p * LOG2E), wkv=b(wkv), wk=b(wk), wvt=b(wvt), vbias=vbias,
        wqn=b(nope.reshape(Q_LORA, N_HEADS * QK_NOPE)),
        wqa_s=b(ropew.reshape(Q_LORA, N_HEADS * QK_ROPE)), wqb_s=b(ropesw.reshape(Q_LORA, N_HEADS * QK_ROPE)),
        bduk=b(bduk), bduv=b(bduv), wo=b(w_o))


def kernel(x_prompt, x_sample, cache_kv_latent, cache_k_rope, state_pool, state_conv, page_table, c_prompt, c_sample, ada_w, ada_b, norm_mix_pre, norm_mix_post, norm_ffn_pre, norm_ffn_post, mla_w_dq, mla_q_norm, mla_w_uq, mla_w_dkv, mla_kv_norm, mla_w_uk, mla_w_uv, mla_w_o, pool_w, pool_scale, ffn_w_up, ffn_conv_w, ffn_conv_b, ffn_w_down):
    nbp, seq, d = x_prompt.shape
    nbs, steps, _ = x_sample.shape
    depth = ada_w.shape[0]
    past = page_table.shape[1] * PAGE_SIZE
    row = lambda a: a.reshape(1, -1)
    tmajor = lambda a: jnp.transpose(a, (1, 0, 2))

    c_all = jnp.concatenate([c_prompt, c_sample], axis=0)
    c_all = jnp.pad(c_all, ((0, (-c_all.shape[0]) % 8), (0, 0)))
    mods = _adaln(c_all, ada_w, ada_b)

    def mods_prompt(i, k0):
        return mods[i, k0:k0 + 3, :nbp, None, :]

    def mods_sample(i, k0):
        return mods[i, k0:k0 + 3, nbp:nbp + nbs]

    wup_all = ffn_w_up.astype(BF16)
    wdn_all = ffn_w_down.astype(BF16)
    xp = x_prompt.reshape(nbp * seq, d)
    xs = tmajor(x_sample)
    nbt = min(SAMPLE_TILE, nbs)

    outs = dict(kv_p=[], kr_p=[], kv_s=[], kr_s=[], pool_p=[], pool_s=[], conv_p=[], conv_s=[])
    for i in range(depth):
        j = i // 2
        if i % 2 == 0:
            w = _mla_weights(mla_w_dq[j], mla_w_uq[j], mla_w_dkv[j], mla_w_uk[j], mla_w_uv[j], mla_w_o[j])
            cos_p, sin_p = _rope_tables(jnp.arange(seq), HEAD_PAD)
            mp = mods_prompt(i, 0)
            q, k, vt, ckv, kr = _proj_prompt(
                xp, mp, row(norm_mix_pre[i]), cos_p, sin_p, w['wdq'], row(mla_q_norm[j]), w['wqa_p'], w['wqb_p'],
                w['wkv'], row(mla_kv_norm[j]), w['wk'], w['wvt'], w['vbias'], seq=seq)
            hq = N_HEADS * HEAD_PAD
            o_t = _flash(q.reshape(nbp, seq, hq), k.reshape(nbp, seq, hq), vt)
            xp = _mla_out(o_t, xp, mp, row(norm_mix_post[i]), w['wo'], seq=seq)
            outs['kv_p'].append(ckv.reshape(nbp, seq, KV_LORA))
            outs['kr_p'].append(kr.reshape(nbp, seq, QK_ROPE))
            cos_k, sin_k = _rope_tables(past + jnp.repeat(jnp.arange(steps), nbt), HEAD_PAD)
            cos_q = jnp.tile(cos_k[:, :QK_ROPE], (1, N_HEADS))
            sin_q = jnp.tile(sin_k[:, :QK_ROPE], (1, N_HEADS))
            ms = mods_sample(i, 0)
            ql, qr, ckv_s, kr_s = _proj_sample(
                xs, ms, row(norm_mix_pre[i]), cos_q, sin_q, cos_k, sin_k, w['wdq'], row(mla_q_norm[j]),
                w['wqn'], w['wqa_s'], w['wqb_s'], w['wkv'], row(mla_kv_norm[j]), w['bduk'])
            ckv_sb = tmajor(ckv_s)
            kr_sb = tmajor(kr_s)
            key_pad = ((0, 0), (0, NEW_KEY_ROWS - steps), (0, 0))
            o_lat = _sattn(page_table, ql.reshape(steps, nbs, N_HEADS, KV_LORA),
                           qr.reshape(steps, nbs, N_HEADS, QK_ROPE),
                           jnp.pad(ckv_sb, key_pad), jnp.pad(kr_sb, key_pad),
                           cache_kv_latent, jnp.swapaxes(cache_k_rope, 2, 3), layer=j, chunk_pages=16)
            xs = _mla_out(o_lat.reshape(steps, nbs, N_HEADS * KV_LORA), xs, ms, row(norm_mix_post[i]),
                          w['wo'], w['bduv'])
            outs['kv_s'].append(ckv_sb)
            outs['kr_s'].append(kr_sb)
        else:
            wg = pool_w[j].astype(BF16)
            xp, st_p = _pool_prompt(xp, mods_prompt(i, 0), row(norm_mix_pre[i]), row(norm_mix_post[i]), wg,
                                    row(pool_scale[j]), seq=seq)
            outs['pool_p'].append(st_p[:, POOL_HALO - POOL_BUF:])
            xs, st_s = _pool_sample(xs, tmajor(state_pool[j]), mods_sample(i, 0), row(norm_mix_pre[i]),
                                    row(norm_mix_post[i]), wg, row(pool_scale[j]), past=past)
            outs['pool_s'].append(tmajor(st_s))

        xp, cst_p = _ffn_prompt(xp, mods_prompt(i, 3), row(norm_ffn_pre[i]), row(norm_ffn_post[i]), wup_all,
                                ffn_conv_w[i], row(ffn_conv_b[i]), wdn_all, layer=i, seq=seq)
        outs['conv_p'].append(cst_p[:, STATE_ROWS - CONV_BUF:])
        xs, cst_s = _ffn_sample(xs, tmajor(state_conv[i]), mods_sample(i, 3), row(norm_ffn_pre[i]),
                                row(norm_ffn_post[i]), wup_all, ffn_conv_w[i], row(ffn_conv_b[i]), wdn_all,
                                layer=i)
        outs['conv_s'].append(tmajor(cst_s))

    st = lambda key: jnp.stack(outs[key])
    return (xp.reshape(nbp, seq, d), tmajor(xs), st('kv_p'), st('kr_p'), st('kv_s'), st('kr_s'),
            st('pool_p'), st('pool_s'), st('conv_p'), st('conv_s'))
```

```python
import functools

import jax
import jax.numpy as jnp
from jax import lax
from jax.experimental import pallas as pl
from jax.experimental.pallas import tpu as pltpu

F32 = jnp.float32
BF16 = jnp.bfloat16

D_MODEL = 1024
N_HEADS = 16
QK_NOPE = 64
QK_ROPE = 32
V_HEAD = 64
Q_LORA = 512
KV_LORA = 256
ROPE_THETA = 10000.0
ATTN_SCALE = (QK_NOPE + QK_ROPE) ** -0.5
PAGE_SIZE = 128
POOL_WINDOWS = (2, 4, 8, 16)
GROUP_W = D_MODEL // len(POOL_WINDOWS)
POOL_BUF = 15
D_FF = 2816
CONV_W = 3
CONV_BUF = CONV_W - 1
EPS = 1e-6

HEAD_PAD = 128
POOL_HALO = 16
CONV_HALO = 16
STATE_ROWS = 8
FF_CHUNK = 256
NEW_KEY_ROWS = 16
PROMPT_TILE = 512
FFN_TILE = 512
SAMPLE_TILE = 64
FLASH_TILE = 256
FLASH_HEADS = 4
VT_ROWS = 80
LOG2E = 1.4426950408889634
VMEM_LIMIT = 56 * 1024 * 1024


def _cparams(sem):
    return pltpu.CompilerParams(dimension_semantics=sem, vmem_limit_bytes=VMEM_LIMIT)


def _dot(a, b):
    return jnp.dot(a, b, preferred_element_type=F32)


def _dot_t(a, b):
    return lax.dot_general(a, b, (((1,), (1,)), ((), ())), preferred_element_type=F32)


def _rms(x, g):
    return x * lax.rsqrt(jnp.mean(x * x, axis=-1, keepdims=True) + EPS) * g


def _modulate(x, g, shift, scale):
    return _rms(x, g) * (1.0 + scale) + shift


def _silu(x):
    return x * (1.0 / (1.0 + jnp.exp(-x)))


def _resident(shape):
    zeros = (0,) * len(shape)
    return pl.BlockSpec(shape, lambda *_: zeros, pipeline_mode=pl.Buffered(1))


def _ld(ref):
    v = ref[...]
    return v if v.ndim == 2 else v.reshape(-1, v.shape[-1])


def _st(ref, val):
    ref[...] = val.reshape(ref.shape).astype(ref.dtype)


def _mod(mod_ref, k, rows):
    m = mod_ref[k]
    if m.shape[0] in (1, rows):
        return m
    return jnp.concatenate([m] * (rows // m.shape[0]), axis=0)


class _Rows:
    def __init__(self, x, seq=None, tile=PROMPT_TILE):
        self.sample = x.ndim == 3
        if self.sample:
            self.steps, self.nb = x.shape[0], x.shape[1]
            self.nbt = min(SAMPLE_TILE, self.nb)
            self.grid = self.nb // self.nbt
            self.tile_rows = self.steps * self.nbt
        else:
            self.seq = seq
            self.tm = min(tile, seq)
            self.grid = x.shape[0] // self.tm
            self.tps = seq // self.tm
            self.tile_rows = self.tm

    def spec(self, w):
        if self.sample:
            return pl.BlockSpec((self.steps, self.nbt, w), lambda i: (0, i, 0))
        return pl.BlockSpec((self.tm, w), lambda i: (i, 0))

    def shape(self, x, w, dtype):
        return jax.ShapeDtypeStruct(x.shape[:-1] + (w,), dtype)

    def mod_spec(self):
        if self.sample:
            return pl.BlockSpec((3, self.nbt, D_MODEL), lambda i: (0, i, 0))
        tps = self.tps
        return pl.BlockSpec((3, None, 1, D_MODEL), lambda i: (0, i // tps, 0, 0))

    def table_spec(self, w):
        if self.sample:
            return _resident((self.tile_rows, w))
        tps = self.tps
        return pl.BlockSpec((self.tm, w), lambda i: (i % tps, 0))


def _adaln_kernel(c_ref, w_ref, b_ref, o_ref):
    a = _silu(c_ref[...]).astype(BF16)
    o_ref[...] = _dot(a, w_ref[...].astype(BF16)) + b_ref[...]


def _adaln(c_all, w, b):
    depth, d, n = w.shape
    rows = c_all.shape[0]
    return pl.pallas_call(
        _adaln_kernel,
        grid=(depth, n // d),
        in_specs=[
            pl.BlockSpec((rows, d), lambda i, j: (0, 0)),
            pl.BlockSpec((None, d, d), lambda i, j: (i, 0, j)),
            pl.BlockSpec((None, 1, d), lambda i, j: (i, 0, j)),
        ],
        out_specs=pl.BlockSpec((None, None, rows, d), lambda i, j: (i, j, 0, 0)),
        out_shape=jax.ShapeDtypeStruct((depth, n // d, rows, d), F32),
        compiler_params=_cparams(("parallel", "parallel")),
        name="adaln",
    )(c_all, w, b.reshape(depth, 1, n))


def _proj_common(x_ref, mod_ref, npre_ref, cosk, sink, wdq_ref, qn_ref, wkv_ref, kvn_ref, ckv_ref, kr_ref):
    x = _ld(x_ref)
    rows = x.shape[0]
    h = _modulate(x, npre_ref[...], _mod(mod_ref, 0, rows), _mod(mod_ref, 1, rows)).astype(BF16)
    cq = _rms(_dot(h, wdq_ref[...]), qn_ref[...]).astype(BF16)
    kv = _dot(h, wkv_ref[...])
    ckv = _rms(kv[:, :KV_LORA], kvn_ref[...])
    krp = kv[:, KV_LORA:KV_LORA + HEAD_PAD] * cosk + kv[:, KV_LORA + HEAD_PAD:] * sink
    _st(ckv_ref, ckv)
    _st(kr_ref, krp[:, :QK_ROPE])
    return cq, ckv, krp


def _proj_prompt_kernel(x_ref, mod_ref, npre_ref, cos_ref, sin_ref, wdq_ref, qn_ref, wqa_ref, wqb_ref,
                        wkv_ref, kvn_ref, wk_ref, wvt_ref, vbias_ref,
                        q_ref, k_ref, vt_ref, ckv_ref, kr_ref):
    cos = cos_ref[...]
    sin = sin_ref[...]
    cq, ckv, krp = _proj_common(x_ref, mod_ref, npre_ref, cos, sin, wdq_ref, qn_ref, wkv_ref, kvn_ref,
                                ckv_ref, kr_ref)
    qa = _dot(cq, wqa_ref[...])
    qb = _dot(cq, wqb_ref[...])
    for hd in range(N_HEADS):
        sl = slice(hd * HEAD_PAD, (hd + 1) * HEAD_PAD)
        q_ref[:, sl] = (qa[:, sl] * cos + qb[:, sl] * sin).astype(q_ref.dtype)
    ckv_b = ckv.astype(BF16)
    kcat = jnp.concatenate([ckv_b, krp.astype(BF16)], axis=-1)
    k_ref[...] = _dot(kcat, wk_ref[...]).astype(k_ref.dtype)
    vt = (_dot_t(wvt_ref[...], ckv_b) + vbias_ref[...]).astype(vt_ref.dtype)
    tk = vt_ref.shape[-1]
    for c in range(vt_ref.shape[0]):
        vt_ref[c] = vt[:, c * tk:(c + 1) * tk]


def _proj_prompt(x2d, mod, npre, cos_t, sin_t, wdq, qn, wqa, wqb, wkv, kvn, wk, wvt, vbias, *, seq):
    r = _Rows(x2d, seq)
    hq = N_HEADS * HEAD_PAD
    tk = min(FLASH_TILE, seq)
    kpt = r.tm // tk
    nbp = x2d.shape[0] // seq
    vt_rows = N_HEADS * VT_ROWS
    return pl.pallas_call(
        _proj_prompt_kernel,
        grid=(r.grid,),
        in_specs=[
            r.spec(D_MODEL), r.mod_spec(), _resident((1, D_MODEL)),
            r.table_spec(HEAD_PAD), r.table_spec(HEAD_PAD),
            _resident(wdq.shape), _resident(qn.shape), _resident(wqa.shape), _resident(wqb.shape),
            _resident(wkv.shape), _resident(kvn.shape), _resident(wk.shape), _resident(wvt.shape),
            _resident(vbias.shape),
        ],
        out_specs=[r.spec(hq), r.spec(hq),
                   pl.BlockSpec((None, kpt, vt_rows, tk), lambda i: (i // r.tps, i % r.tps, 0, 0)),
                   r.spec(KV_LORA), r.spec(QK_ROPE)],
        out_shape=[r.shape(x2d, hq, BF16), r.shape(x2d, hq, BF16),
                   jax.ShapeDtypeStruct((nbp, seq // tk, vt_rows, tk), BF16),
                   r.shape(x2d, KV_LORA, F32), r.shape(x2d, QK_ROPE, F32)],
        compiler_params=_cparams(("parallel",)),
        name="mla_proj_prompt",
    )(x2d, mod, npre, cos_t, sin_t, wdq, qn, wqa, wqb, wkv, kvn, wk, wvt, vbias)


def _proj_sample_kernel(x_ref, mod_ref, npre_ref, cosq_ref, sinq_ref, cosk_ref, sink_ref,
                        wdq_ref, qn_ref, wqn_ref, wqa_ref, wqb_ref, wkv_ref, kvn_ref, bduk_ref,
                        ql_ref, qr_ref, ckv_ref, kr_ref):
    cq, _, _ = _proj_common(x_ref, mod_ref, npre_ref, cosk_ref[...], sink_ref[...], wdq_ref, qn_ref, wkv_ref,
                            kvn_ref, ckv_ref, kr_ref)
    q_nope = _dot(cq, wqn_ref[...]).astype(BF16)
    _st(ql_ref, _dot(q_nope, bduk_ref[...]))
    _st(qr_ref, _dot(cq, wqa_ref[...]) * cosq_ref[...] + _dot(cq, wqb_ref[...]) * sinq_ref[...])


def _proj_sample(x3d, mod, npre, cosq, sinq, cosk, sink, wdq, qn, wqn, wqa, wqb, wkv, kvn, bduk):
    r = _Rows(x3d)
    return pl.pallas_call(
        _proj_sample_kernel,
        grid=(r.grid,),
        in_specs=[
            r.spec(D_MODEL), r.mod_spec(), _resident((1, D_MODEL)),
            r.table_spec(N_HEADS * QK_ROPE), r.table_spec(N_HEADS * QK_ROPE),
            r.table_spec(HEAD_PAD), r.table_spec(HEAD_PAD),
            _resident(wdq.shape), _resident(qn.shape), _resident(wqn.shape), _resident(wqa.shape),
            _resident(wqb.shape), _resident(wkv.shape), _resident(kvn.shape), _resident(bduk.shape),
        ],
        out_specs=[r.spec(N_HEADS * KV_LORA), r.spec(N_HEADS * QK_ROPE), r.spec(KV_LORA), r.spec(QK_ROPE)],
        out_shape=[r.shape(x3d, N_HEADS * KV_LORA, BF16), r.shape(x3d, N_HEADS * QK_ROPE, BF16),
                   r.shape(x3d, KV_LORA, F32), r.shape(x3d, QK_ROPE, F32)],
        compiler_params=_cparams(("parallel",)),
        name="mla_proj_sample",
    )(x3d, mod, npre, cosq, sinq, cosk, sink, wdq, qn, wqn, wqa, wqb, wkv, kvn, bduk)


def _flash_kernel(q_ref, k_ref, vt_ref, o_ref, sa_ref, sb_ref, m_ref, acc_ref, *, tq, tk, heads):
    qi = pl.program_id(2)
    below = lax.broadcasted_iota(jnp.int32, (tq, tq), 0) <= lax.broadcasted_iota(jnp.int32, (tq, tq), 1)
    qs = [q_ref[:, h * HEAD_PAD:(h + 1) * HEAD_PAD] for h in range(heads)]

    def scores(j, s_ref):
        k0 = pl.multiple_of(j * tq, tq)
        for h in range(heads):
            s_ref[h] = _dot_t(k_ref[pl.ds(k0, tq), h * HEAD_PAD:(h + 1) * HEAD_PAD], qs[h])

    def consume(j, s_ref, diagonal):
        pts, ms, alphas = [], [], []
        for h in range(heads):
            st = s_ref[h]
            if diagonal:
                st = jnp.where(below, st, -jnp.inf)
            m_old = m_ref[h]
            m_new = jnp.maximum(m_old, jnp.max(st, axis=0, keepdims=True))
            alphas.append(jnp.exp2(m_old - m_new))
            pts.append(jnp.exp2(st - m_new).astype(BF16))
            ms.append(m_new)
        for h in range(heads):
            rows = slice(h * VT_ROWS, (h + 1) * VT_ROWS)
            pv = sum(_dot(vt_ref[(tq // tk) * j + c, rows, :], pts[h][c * tk:(c + 1) * tk])
                     for c in range(tq // tk))
            acc_ref[h] = alphas[h] * acc_ref[h] + pv
            m_ref[h] = ms[h]

    m_ref[...] = jnp.full(m_ref.shape, -jnp.inf, F32)
    acc_ref[...] = jnp.zeros(acc_ref.shape, F32)
    scores(0, sa_ref)

    def two_spans(t, c):
        scores(2 * t + 1, sb_ref)
        consume(2 * t, sa_ref, False)
        scores(2 * t + 2, sa_ref)
        consume(2 * t + 1, sb_ref, False)
        return c

    lax.fori_loop(0, qi // 2, two_spans, 0)

    @pl.when(qi % 2 == 0)
    def _():
        consume(qi, sa_ref, True)

    @pl.when(qi % 2 == 1)
    def _():
        scores(qi, sb_ref)
        consume(qi - 1, sa_ref, False)
        consume(qi, sb_ref, True)

    for h in range(heads):
        acc = acc_ref[h]
        o_ref[h * V_HEAD:(h + 1) * V_HEAD, :] = (acc[:V_HEAD] / acc[V_HEAD:V_HEAD + 1]).astype(o_ref.dtype)


def _flash(q, k, vt):
    nb, seq, _ = q.shape
    tk = vt.shape[-1]
    tq = 2 * tk
    heads = FLASH_HEADS
    assert seq % tq == 0 and N_HEADS % heads == 0
    return pl.pallas_call(
        functools.partial(_flash_kernel, tq=tq, tk=tk, heads=heads),
        grid=(nb, N_HEADS // heads, seq // tq),
        in_specs=[
            pl.BlockSpec((None, tq, heads * HEAD_PAD), lambda b, g, i: (b, i, g)),
            pl.BlockSpec((None, seq, heads * HEAD_PAD), lambda b, g, i: (b, 0, g)),
            pl.BlockSpec((None, seq // tk, heads * VT_ROWS, tk), lambda b, g, i: (b, 0, g, 0)),
        ],
        out_specs=pl.BlockSpec((None, heads * V_HEAD, tq), lambda b, g, i: (b, g, i)),
        out_shape=jax.ShapeDtypeStruct((nb, N_HEADS * V_HEAD, seq), BF16),
        scratch_shapes=[pltpu.VMEM((heads, tq, tq), F32), pltpu.VMEM((heads, tq, tq), F32),
                        pltpu.VMEM((heads, 1, tq), F32), pltpu.VMEM((heads, VT_ROWS, tq), F32)],
        compiler_params=_cparams(("parallel", "parallel", "arbitrary")),
        name="flash_prompt",
    )(q, k, vt)


def _sattn_kernel(pt_ref, ql_ref, qr_ref, cn_ref, rn_ref, ckv_hbm, krt_hbm, o_ref, kvbuf, krbuf, sem,
                  s_a, s_b, kv_a, kv_b, *, layer, n_pages, chunk_pages):
    b = pl.program_id(0)
    nb = pl.num_programs(0)
    slot = b % 2

    def page_copies(bb, sl, p):
        page = pt_ref[bb * n_pages + p]
        return (pltpu.make_async_copy(ckv_hbm.at[layer, page], kvbuf.at[sl, p], sem.at[0, sl]),
                pltpu.make_async_copy(krt_hbm.at[layer, page], krbuf.at[sl, p], sem.at[1, sl]))

    def start_fetch(bb, sl):
        for p in range(n_pages):
            for cp in page_copies(bb, sl, p):
                cp.start()

    def wait_fetch(bb, sl):
        for p in range(n_pages):
            for cp in page_copies(bb, sl, p):
                cp.wait()

    @pl.when(b == 0)
    def _():
        start_fetch(0, 0)

    @pl.when(b + 1 < nb)
    def _():
        start_fetch(b + 1, 1 - slot)

    wait_fetch(b, slot)

    ql = ql_ref[...].reshape(-1, KV_LORA)
    qr = qr_ref[...].reshape(-1, QK_ROPE)
    rows = ql.shape[0]
    ck = chunk_pages * PAGE_SIZE

    s_bufs = (s_a, s_b)
    kv_bufs = (kv_a, kv_b)

    def scores(c):
        p0 = c * chunk_pages
        kv = kvbuf[slot, p0:p0 + chunk_pages].reshape(ck, KV_LORA).astype(BF16)
        kv_bufs[c % 2][...] = kv
        krt = jnp.concatenate([krbuf[slot, p0 + i] for i in range(chunk_pages)], axis=1).astype(BF16)
        s_bufs[c % 2][...] = _dot_t(ql, kv) + _dot(qr, krt)

    def softmax_step(s, values, carry):
        m, l, acc = carry
        m_new = jnp.maximum(m, jnp.max(s, axis=-1, keepdims=True))
        alpha = jnp.exp(m - m_new)
        p = jnp.exp(s - m_new)
        l = alpha * l + jnp.sum(p, axis=-1, keepdims=True)
        return m_new, l, alpha * acc + _dot(p.astype(BF16), values)

    carry = (jnp.full((rows, 1), -jnp.inf, F32), jnp.zeros((rows, 1), F32), jnp.zeros((rows, KV_LORA), F32))
    n_chunks = n_pages // chunk_pages
    scores(0)
    for c in range(n_chunks):
        if c + 1 < n_chunks:
            scores(c + 1)
        carry = softmax_step(s_bufs[c % 2][...], kv_bufs[c % 2][...], carry)

    cn = cn_ref[...].astype(BF16)
    rn = rn_ref[...].astype(BF16)
    s = _dot_t(ql, cn) + _dot_t(qr, rn)
    qstep = lax.broadcasted_iota(jnp.int32, s.shape, 0) // N_HEADS
    kstep = lax.broadcasted_iota(jnp.int32, s.shape, 1)
    s = jnp.where(kstep <= qstep, s, -jnp.inf)
    _, l, acc = softmax_step(s, cn, carry)
    o_ref[...] = (acc / l).astype(o_ref.dtype).reshape(o_ref.shape)


def _sattn(page_table, ql, qr, cn, rn, cache_kv, cache_krt, *, layer, chunk_pages):
    steps, nb = ql.shape[0], ql.shape[1]
    rows = steps * N_HEADS
    n_pages = page_table.shape[1]
    n_new = cn.shape[1]
    ck = chunk_pages * PAGE_SIZE
    qspec = lambda w: pl.BlockSpec((steps, None, N_HEADS, w), lambda b, pt: (0, b, 0, 0))
    grid_spec = pltpu.PrefetchScalarGridSpec(
        num_scalar_prefetch=1,
        grid=(nb,),
        in_specs=[
            qspec(KV_LORA),
            qspec(QK_ROPE),
            pl.BlockSpec((None, n_new, KV_LORA), lambda b, pt: (b, 0, 0)),
            pl.BlockSpec((None, n_new, QK_ROPE), lambda b, pt: (b, 0, 0)),
            pl.BlockSpec(memory_space=pl.ANY),
            pl.BlockSpec(memory_space=pl.ANY),
        ],
        out_specs=qspec(KV_LORA),
        scratch_shapes=[
            pltpu.VMEM((2, n_pages, PAGE_SIZE, KV_LORA), F32),
            pltpu.VMEM((2, n_pages, QK_ROPE, PAGE_SIZE), F32),
            pltpu.SemaphoreType.DMA((2, 2)),
            pltpu.VMEM((rows, ck), F32), pltpu.VMEM((rows, ck), F32),
            pltpu.VMEM((ck, KV_LORA), BF16), pltpu.VMEM((ck, KV_LORA), BF16),
        ],
    )
    return pl.pallas_call(
        functools.partial(_sattn_kernel, layer=layer, n_pages=n_pages, chunk_pages=chunk_pages),
        grid_spec=grid_spec,
        out_shape=jax.ShapeDtypeStruct((steps, nb, N_HEADS, KV_LORA), BF16),
        compiler_params=_cparams(("arbitrary",)),
        name="sample_attend",
    )(page_table.reshape(-1), ql, qr, cn, rn, cache_kv, cache_krt)


def _mla_out_kernel(*refs, absorbed):
    if absorbed:
        o_ref, x_ref, mod_ref, npost_ref, bduv_ref, wo_ref, y_ref = refs
        o = _dot(_ld(o_ref), bduv_ref[...]).astype(BF16)
        f = _dot(o, wo_ref[...])
    else:
        o_ref, x_ref, mod_ref, npost_ref, wo_ref, y_ref = refs
        f = lax.dot_general(o_ref[...], wo_ref[...], (((0,), (0,)), ((), ())), preferred_element_type=F32)
    x = _ld(x_ref)
    _st(y_ref, x + _mod(mod_ref, 2, x.shape[0]) * _rms(f, npost_ref[...]))


def _mla_out(o, x, mod, npost, wo, bduv=None, *, seq=None):
    r = _Rows(x, seq)
    if bduv is not None:
        o_spec = r.spec(o.shape[-1])
    else:
        o_spec = pl.BlockSpec((None, o.shape[1], r.tm), lambda i: (i // r.tps, 0, i % r.tps))
    in_specs = [o_spec, r.spec(D_MODEL), r.mod_spec(), _resident((1, D_MODEL))]
    args = [o, x, mod, npost]
    if bduv is not None:
        in_specs.append(_resident(bduv.shape))
        args.append(bduv)
    in_specs.append(_resident(wo.shape))
    args.append(wo)
    return pl.pallas_call(
        functools.partial(_mla_out_kernel, absorbed=bduv is not None),
        grid=(r.grid,),
        in_specs=in_specs,
        out_specs=r.spec(D_MODEL),
        out_shape=r.shape(x, D_MODEL, F32),
        compiler_params=_cparams(("parallel",)),
        name="mla_out_sample" if bduv is not None else "mla_out_prompt",
    )(*args)


def _pool_finish(window_means, h, x, mod_ref, npost_ref, wg_ref, ls_ref, y_ref):
    outs = []
    for g in range(len(POOL_WINDOWS)):
        cols = slice(g * GROUP_W, (g + 1) * GROUP_W)
        pooled = (window_means[g] - h[:, cols]).astype(BF16)
        outs.append(_dot(pooled, wg_ref[g]))
    o = jnp.concatenate(outs, axis=-1) * ls_ref[...]
    _st(y_ref, x + _mod(mod_ref, 2, x.shape[0]) * _rms(o, npost_ref[...]))


def _pool_prompt_kernel(x_ref, xh_ref, mod_ref, npre_ref, npost_ref, wg_ref, ls_ref, y_ref, st_ref, hext_ref,
                        *, tm, tps):
    t_in_seq = pl.program_id(0) % tps
    keep = jnp.where(t_in_seq == 0, 0.0, 1.0)
    x = x_ref[...]
    h = _modulate(x, npre_ref[...], mod_ref[0], mod_ref[1])
    hh = _modulate(xh_ref[...], npre_ref[...], mod_ref[0], mod_ref[1]) * keep
    hext_ref[0:POOL_HALO, :] = hh
    hext_ref[POOL_HALO:, :] = h
    st_ref[...] = h[tm - POOL_HALO:, :]
    pos = t_in_seq * tm + lax.broadcasted_iota(jnp.int32, (tm, 1), 0)
    means = []
    for g, w in enumerate(POOL_WINDOWS):
        cols = slice(g * GROUP_W, (g + 1) * GROUP_W)
        s = h[:, cols]
        for j in range(1, w):
            s = s + hext_ref[POOL_HALO - j:POOL_HALO - j + tm, cols]
        means.append(s / jnp.minimum(pos + 1, w).astype(F32))
    _pool_finish(means, h, x, mod_ref, npost_ref, wg_ref, ls_ref, y_ref)


def _pool_prompt(x2d, mod, npre, npost, wg, ls, *, seq):
    r = _Rows(x2d, seq)
    hb = r.tm // POOL_HALO
    return pl.pallas_call(
        functools.partial(_pool_prompt_kernel, tm=r.tm, tps=r.tps),
        grid=(r.grid,),
        in_specs=[
            r.spec(D_MODEL),
            pl.BlockSpec((POOL_HALO, D_MODEL), lambda i: (jnp.maximum(i * hb - 1, 0), 0)),
            r.mod_spec(),
            _resident((1, D_MODEL)), _resident((1, D_MODEL)), _resident(wg.shape), _resident((1, D_MODEL)),
        ],
        out_specs=[r.spec(D_MODEL), pl.BlockSpec((None, POOL_HALO, D_MODEL), lambda i: (i // r.tps, 0, 0))],
        out_shape=[r.shape(x2d, D_MODEL, F32),
                   jax.ShapeDtypeStruct((x2d.shape[0] // seq, POOL_HALO, D_MODEL), F32)],
        scratch_shapes=[pltpu.VMEM((r.tm + POOL_HALO, D_MODEL), F32)],
        compiler_params=_cparams(("arbitrary",)),
        name="pool_prompt",
    )(x2d, x2d, mod, npre, npost, wg, ls)


def _pool_sample_kernel(x_ref, st_ref, mod_ref, npre_ref, npost_ref, wg_ref, ls_ref, y_ref, so_ref, *, past):
    steps, nbt, _ = x_ref.shape
    x = _ld(x_ref)
    h = _modulate(x, npre_ref[...], _mod(mod_ref, 0, x.shape[0]), _mod(mod_ref, 1, x.shape[0]))

    def slab(k, cols):
        if k < POOL_BUF:
            return st_ref[k, :, cols]
        return h[(k - POOL_BUF) * nbt:(k - POOL_BUF + 1) * nbt, cols]

    for k in range(POOL_BUF):
        so_ref[k] = slab(k + steps, slice(None))
    means = []
    for g, w in enumerate(POOL_WINDOWS):
        cols = slice(g * GROUP_W, (g + 1) * GROUP_W)
        parts = []
        for s_ in range(steps):
            acc = slab(POOL_BUF + s_, cols)
            for j in range(1, w):
                acc = acc + slab(POOL_BUF + s_ - j, cols)
            parts.append(acc / float(min(past + s_ + 1, w)))
        means.append(jnp.concatenate(parts, axis=0))
    _pool_finish(means, h, x, mod_ref, npost_ref, wg_ref, ls_ref, y_ref)


def _pool_sample(x3d, st3d, mod, npre, npost, wg, ls, *, past):
    r = _Rows(x3d)
    st_spec = pl.BlockSpec((POOL_BUF, r.nbt, D_MODEL), lambda i: (0, i, 0))
    return pl.pallas_call(
        functools.partial(_pool_sample_kernel, past=past),
        grid=(r.grid,),
        in_specs=[r.spec(D_MODEL), st_spec, r.mod_spec(),
                  _resident((1, D_MODEL)), _resident((1, D_MODEL)), _resident(wg.shape), _resident((1, D_MODEL))],
        out_specs=[r.spec(D_MODEL), st_spec],
        out_shape=[r.shape(x3d, D_MODEL, F32), jax.ShapeDtypeStruct(st3d.shape, F32)],
        compiler_params=_cparams(("parallel",)),
        name="pool_sample",
    )(x3d, st3d, mod, npre, npost, wg, ls)


def _ffn_chunks(h, bufs, put_fn, conv_fn, cw_ref, cb_ref, wup_ref, wdn_ref):
    n = D_FF // FF_CHUNK
    cols = lambda c, off: slice(off + c * FF_CHUNK, off + (c + 1) * FF_CHUNK)

    def up(c):
        for k, off in enumerate((0, D_FF)):
            put_fn(bufs[c % 2], k, _dot(h, wup_ref[:, cols(c, off)]))

    up(0)
    acc = None
    for c in range(n):
        halves = []
        for k, off in enumerate((0, D_FF)):
            u, r1, r2 = conv_fn(bufs[c % 2], k, cols(c, off))
            cw = cw_ref[:, cols(c, off)]
            halves.append(cb_ref[:, cols(c, off)] + r2 * cw[0:1] + r1 * cw[1:2] + u * cw[2:3])
        if c + 1 < n:
            up(c + 1)
        gate = (_silu(halves[0]) * halves[1]).astype(BF16)
        part = _dot(gate, wdn_ref[c * FF_CHUNK:(c + 1) * FF_CHUNK, :])
        acc = part if acc is None else acc + part
    return acc


def _ffn_prompt_kernel(x_ref, xh_ref, mod_ref, npre_ref, npost_ref, wup_ref, cw_ref, cb_ref, wdn_ref,
                       y_ref, st_ref, buf_a, buf_b, *, tm, tps):
    keep = jnp.where(pl.program_id(0) % tps == 0, 0.0, 1.0)
    x = x_ref[...]
    xe = jnp.concatenate([xh_ref[...], x], axis=0)
    he = _modulate(xe, npre_ref[...], mod_ref[0], mod_ref[1]).astype(BF16)
    hl = CONV_HALO

    def put_fn(buf, k, ue):
        buf[k, :hl] = ue[:hl] * keep
        buf[k, hl:] = ue[hl:]

    def conv_fn(buf, k, cols):
        st_ref[:, cols] = buf[k, hl + tm - STATE_ROWS:hl + tm]
        return buf[k, hl:hl + tm], buf[k, hl - 1:hl - 1 + tm], buf[k, hl - 2:hl - 2 + tm]

    f = _ffn_chunks(he, (buf_a, buf_b), put_fn, conv_fn, cw_ref, cb_ref, wup_ref, wdn_ref)
    y_ref[...] = x + mod_ref[2] * _rms(f, npost_ref[...])


def _ffn_prompt(x2d, mod, npre, npost, wup, cw, cb, wdn, *, layer, seq):
    r = _Rows(x2d, seq, FFN_TILE)
    hb = r.tm // CONV_HALO
    lyr = lambda a: pl.BlockSpec((None,) + a.shape[1:], lambda i: (layer,) + (0,) * (a.ndim - 1),
                                 pipeline_mode=pl.Buffered(1))
    return pl.pallas_call(
        functools.partial(_ffn_prompt_kernel, tm=r.tm, tps=r.tps),
        grid=(r.grid,),
        in_specs=[
            r.spec(D_MODEL),
            pl.BlockSpec((CONV_HALO, D_MODEL), lambda i: (jnp.maximum(i * hb - 1, 0), 0)),
            r.mod_spec(),
            _resident((1, D_MODEL)), _resident((1, D_MODEL)),
            lyr(wup), _resident(cw.shape), _resident(cb.shape), lyr(wdn),
        ],
        out_specs=[r.spec(D_MODEL), pl.BlockSpec((None, STATE_ROWS, 2 * D_FF), lambda i: (i // r.tps, 0, 0))],
        out_shape=[r.shape(x2d, D_MODEL, F32),
                   jax.ShapeDtypeStruct((x2d.shape[0] // seq, STATE_ROWS, 2 * D_FF), F32)],
        scratch_shapes=[pltpu.VMEM((2, r.tm + CONV_HALO, FF_CHUNK), F32)] * 2,
        compiler_params=_cparams(("arbitrary",)),
        name="ffn_prompt",
    )(x2d, x2d, mod, npre, npost, wup, cw, cb, wdn)


def _ffn_sample_kernel(x_ref, st_ref, mod_ref, npre_ref, npost_ref, wup_ref, cw_ref, cb_ref, wdn_ref,
                       y_ref, so_ref, buf_a, buf_b):
    steps, nbt, _ = x_ref.shape
    x = _ld(x_ref)
    rows = x.shape[0]
    h = _modulate(x, npre_ref[...], _mod(mod_ref, 0, rows), _mod(mod_ref, 1, rows)).astype(BF16)

    def put_fn(buf, k, u):
        buf[k] = u

    def conv_fn(buf, k, cols):
        s0 = st_ref[0, :, cols]
        s1 = st_ref[1, :, cols]
        so_ref[0, :, cols] = buf[k, rows - 2 * nbt:rows - nbt]
        so_ref[1, :, cols] = buf[k, rows - nbt:rows]
        return (buf[k], jnp.concatenate([s1, buf[k, :rows - nbt]], axis=0),
                jnp.concatenate([s0, s1, buf[k, :rows - 2 * nbt]], axis=0))

    f = _ffn_chunks(h, (buf_a, buf_b), put_fn, conv_fn, cw_ref, cb_ref, wup_ref, wdn_ref)
    _st(y_ref, x + _mod(mod_ref, 2, rows) * _rms(f, npost_ref[...]))


def _ffn_sample(x3d, st3d, mod, npre, npost, wup, cw, cb, wdn, *, layer):
    r = _Rows(x3d)
    st_spec = pl.BlockSpec((CONV_BUF, r.nbt, 2 * D_FF), lambda i: (0, i, 0))
    lyr = lambda a: pl.BlockSpec((None,) + a.shape[1:], lambda i: (layer,) + (0,) * (a.ndim - 1),
                                 pipeline_mode=pl.Buffered(1))
    return pl.pallas_call(
        _ffn_sample_kernel,
        grid=(r.grid,),
        in_specs=[r.spec(D_MODEL), st_spec, r.mod_spec(),
                  _resident((1, D_MODEL)), _resident((1, D_MODEL)),
                  lyr(wup), _resident(cw.shape), _resident(cb.shape), lyr(wdn)],
        out_specs=[r.spec(D_MODEL), st_spec],
        out_shape=[r.shape(x3d, D_MODEL, F32), jax.ShapeDtypeStruct(st3d.shape, F32)],
        scratch_shapes=[pltpu.VMEM((2, r.tile_rows, FF_CHUNK), F32)] * 2,
        compiler_params=_cparams(("parallel",)),
        name="ffn_sample",
    )(x3d, st3d, mod, npre, npost, wup, cw, cb, wdn)


def _rope_angles(pos):
    half = QK_ROPE // 2
    inv = ROPE_THETA ** (-jnp.arange(half, dtype=F32) / half)
    ang = pos.astype(F32)[:, None] * inv[None, :]
    return jnp.cos(ang), jnp.sin(ang)


def _rope_tables(pos, width):
    cos, sin = _rope_angles(pos)
    n = pos.shape[0]
    cos_t = jnp.concatenate([cos, cos, jnp.ones((n, width - QK_ROPE), F32)], axis=-1)
    sin_t = jnp.concatenate([-sin, sin, jnp.zeros((n, width - QK_ROPE), F32)], axis=-1)
    return cos_t, sin_t


def _swap_halves(w):
    half = QK_ROPE // 2
    return jnp.concatenate([w[..., half:], w[..., :half]], axis=-1)


def _mla_weights(w_dq, w_uq, w_dkv, w_uk, w_uv, w_o):
    wq3 = w_uq.reshape(Q_LORA, N_HEADS, QK_NOPE + QK_ROPE) * ATTN_SCALE
    nope, ropew = wq3[..., :QK_NOPE], wq3[..., QK_NOPE:]
    ropesw = _swap_halves(ropew)
    zpad = jnp.zeros((Q_LORA, N_HEADS, HEAD_PAD - QK_NOPE - QK_ROPE), F32)
    wqa_p = jnp.concatenate([ropew, nope, zpad], axis=-1).reshape(Q_LORA, N_HEADS * HEAD_PAD)
    wqb_p = jnp.concatenate([ropesw, jnp.zeros((Q_LORA, N_HEADS, HEAD_PAD - QK_ROPE), F32)], axis=-1)
    wqb_p = wqb_p.reshape(Q_LORA, N_HEADS * HEAD_PAD)
    kr_w = w_dkv[:, KV_LORA:]
    lane_pad = jnp.zeros((D_MODEL, HEAD_PAD - QK_ROPE), F32)
    wkv = jnp.concatenate([w_dkv[:, :KV_LORA], kr_w, lane_pad, _swap_halves(kr_w), lane_pad], axis=-1)
    eye_r = jnp.eye(QK_ROPE, dtype=F32)
    top = jnp.concatenate([jnp.zeros((KV_LORA, N_HEADS, QK_ROPE), F32), w_uk,
                           jnp.zeros((KV_LORA, N_HEADS, HEAD_PAD - QK_NOPE - QK_ROPE), F32)], axis=-1)
    mid = jnp.concatenate([jnp.broadcast_to(eye_r[:, None, :], (QK_ROPE, N_HEADS, QK_ROPE)),
                           jnp.zeros((QK_ROPE, N_HEADS, HEAD_PAD - QK_ROPE), F32)], axis=-1)
    bot = jnp.zeros((HEAD_PAD - QK_ROPE, N_HEADS, HEAD_PAD), F32)
    wk = jnp.concatenate([top, mid, bot], axis=0).reshape(KV_LORA + HEAD_PAD, N_HEADS * HEAD_PAD)
    wvt = jnp.pad(jnp.transpose(w_uv, (1, 2, 0)), ((0, 0), (0, VT_ROWS - V_HEAD), (0, 0)))
    wvt = wvt.reshape(N_HEADS * VT_ROWS, KV_LORA)
    vbias = jnp.tile((jnp.arange(VT_ROWS) == V_HEAD).astype(F32), N_HEADS).reshape(N_HEADS * VT_ROWS, 1)
    eye_h = jnp.eye(N_HEADS, dtype=F32)
    bduk = jnp.einsum('hg,khn->hngk', eye_h, w_uk).reshape(N_HEADS * QK_NOPE, N_HEADS * KV_LORA)
    bduv = jnp.einsum('hg,khv->hkgv', eye_h, w_uv).reshape(N_HEADS * KV_LORA, N_HEADS * V_HEAD)
    b = lambda a: a.astype(BF16)
    return dict(
        wdq=b(w_dq), wqa_p=b(wqa_p * LOG2E), wqb_p=b(wqb_p * LOG2E), wkv=b(wkv), wk=b(wk), wvt=b(wvt), vbias=vbias,
        wqn=b(nope.reshape(Q_LORA, N_HEADS * QK_NOPE)),
        wqa_s=b(ropew.reshape(Q_LORA, N_HEADS * QK_ROPE)), wqb_s=b(ropesw.reshape(Q_LORA, N_HEADS * QK_ROPE)),
        bduk=b(bduk), bduv=b(bduv), wo=b(w_o))


def kernel(x_prompt, x_sample, cache_kv_latent, cache_k_rope, state_pool, state_conv, page_table, c_prompt, c_sample, ada_w, ada_b, norm_mix_pre, norm_mix_post, norm_ffn_pre, norm_ffn_post, mla_w_dq, mla_q_norm, mla_w_uq, mla_w_dkv, mla_kv_norm, mla_w_uk, mla_w_uv, mla_w_o, pool_w, pool_scale, ffn_w_up, ffn_conv_w, ffn_conv_b, ffn_w_down):
    nbp, seq, d = x_prompt.shape
    nbs, steps, _ = x_sample.shape
    depth = ada_w.shape[0]
    past = page_table.shape[1] * PAGE_SIZE
    row = lambda a: a.reshape(1, -1)
    tmajor = lambda a: jnp.transpose(a, (1, 0, 2))

    c_all = jnp.concatenate([c_prompt, c_sample], axis=0)
    c_all = jnp.pad(c_all, ((0, (-c_all.shape[0]) % 8), (0, 0)))
    mods = _adaln(c_all, ada_w, ada_b)

    def mods_prompt(i, k0):
        return mods[i, k0:k0 + 3, :nbp, None, :]

    def mods_sample(i, k0):
        return mods[i, k0:k0 + 3, nbp:nbp + nbs]

    wup_all = ffn_w_up.astype(BF16)
    wdn_all = ffn_w_down.astype(BF16)
    xp = x_prompt.reshape(nbp * seq, d)
    xs = tmajor(x_sample)
    nbt = min(SAMPLE_TILE, nbs)

    outs = dict(kv_p=[], kr_p=[], kv_s=[], kr_s=[], pool_p=[], pool_s=[], conv_p=[], conv_s=[])
    for i in range(depth):
        j = i // 2
        if i % 2 == 0:
            w = _mla_weights(mla_w_dq[j], mla_w_uq[j], mla_w_dkv[j], mla_w_uk[j], mla_w_uv[j], mla_w_o[j])
            cos_p, sin_p = _rope_tables(jnp.arange(seq), HEAD_PAD)
            mp = mods_prompt(i, 0)
            q, k, vt, ckv, kr = _proj_prompt(
                xp, mp, row(norm_mix_pre[i]), cos_p, sin_p, w['wdq'], row(mla_q_norm[j]), w['wqa_p'], w['wqb_p'],
                w['wkv'], row(mla_kv_norm[j]), w['wk'], w['wvt'], w['vbias'], seq=seq)
            hq = N_HEADS * HEAD_PAD
            o_t = _flash(q.reshape(nbp, seq, hq), k.reshape(nbp, seq, hq), vt)
            xp = _mla_out(o_t, xp, mp, row(norm_mix_post[i]), w['wo'], seq=seq)
            outs['kv_p'].append(ckv.reshape(nbp, seq, KV_LORA))
            outs['kr_p'].append(kr.reshape(nbp, seq, QK_ROPE))
            cos_k, sin_k = _rope_tables(past + jnp.repeat(jnp.arange(steps), nbt), HEAD_PAD)
            cos_q = jnp.tile(cos_k[:, :QK_ROPE], (1, N_HEADS))
            sin_q = jnp.tile(sin_k[:, :QK_ROPE], (1, N_HEADS))
            ms = mods_sample(i, 0)
            ql, qr, ckv_s, kr_s = _proj_sample(
                xs, ms, row(norm_mix_pre[i]), cos_q, sin_q, cos_k, sin_k, w['wdq'], row(mla_q_norm[j]),
                w['wqn'], w['wqa_s'], w['wqb_s'], w['wkv'], row(mla_kv_norm[j]), w['bduk'])
            ckv_sb = tmajor(ckv_s)
            kr_sb = tmajor(kr_s)
            key_pad = ((0, 0), (0, NEW_KEY_ROWS - steps), (0, 0))
            o_lat = _sattn(page_table, ql.reshape(steps, nbs, N_HEADS, KV_LORA),
                           qr.reshape(steps, nbs, N_HEADS, QK_ROPE),
                           jnp.pad(ckv_sb, key_pad), jnp.pad(kr_sb, key_pad),
                           cache_kv_latent, jnp.swapaxes(cache_k_rope, 2, 3), layer=j, chunk_pages=16)
            xs = _mla_out(o_lat.reshape(steps, nbs, N_HEADS * KV_LORA), xs, ms, row(norm_mix_post[i]),
                          w['wo'], w['bduv'])
            outs['kv_s'].append(ckv_sb)
            outs['kr_s'].append(kr_sb)
        else:
            wg = pool_w[j].astype(BF16)
            xp, st_p = _pool_prompt(xp, mods_prompt(i, 0), row(norm_mix_pre[i]), row(norm_mix_post[i]), wg,
                                    row(pool_scale[j]), seq=seq)
            outs['pool_p'].append(st_p[:, POOL_HALO - POOL_BUF:])
            xs, st_s = _pool_sample(xs, tmajor(state_pool[j]), mods_sample(i, 0), row(norm_mix_pre[i]),
                                    row(norm_mix_post[i]), wg, row(pool_scale[j]), past=past)
            outs['pool_s'].append(tmajor(st_s))

        xp, cst_p = _ffn_prompt(xp, mods_prompt(i, 3), row(norm_ffn_pre[i]), row(norm_ffn_post[i]), wup_all,
                                ffn_conv_w[i], row(ffn_conv_b[i]), wdn_all, layer=i, seq=seq)
        outs['conv_p'].append(cst_p[:, STATE_ROWS - CONV_BUF:])
        xs, cst_s = _ffn_sample(xs, tmajor(state_conv[i]), mods_sample(i, 3), row(norm_ffn_pre[i]),
                                row(norm_ffn_post[i]), wup_all, ffn_conv_w[i], row(ffn_conv_b[i]), wdn_all,
                                layer=i)
        outs['conv_s'].append(tmajor(cst_s))

    st = lambda key: jnp.stack(outs[key])
    return (xp.reshape(nbp, seq, d), tmajor(xs), st('kv_p'), st('kr_p'), st('kv_s'), st('kr_s'),
            st('pool_p'), st('pool_s'), st('conv_p'), st('conv_s'))
```

```python
import functools

import jax
import jax.numpy as jnp
from jax import lax
from jax.experimental import pallas as pl
from jax.experimental.pallas import tpu as pltpu

F32 = jnp.float32
BF16 = jnp.bfloat16

D_MODEL = 1024
N_HEADS = 16
QK_NOPE = 64
QK_ROPE = 32
V_HEAD = 64
Q_LORA = 512
KV_LORA = 256
ROPE_THETA = 10000.0
ATTN_SCALE = (QK_NOPE + QK_ROPE) ** -0.5
PAGE_SIZE = 128
POOL_WINDOWS = (2, 4, 8, 16)
GROUP_W = D_MODEL // len(POOL_WINDOWS)
POOL_BUF = 15
D_FF = 2816
CONV_W = 3
CONV_BUF = CONV_W - 1
EPS = 1e-6

HEAD_PAD = 128
POOL_HALO = 16
CONV_HALO = 16
STATE_ROWS = 8
FF_CHUNK = 256
NEW_KEY_ROWS = 16
PROMPT_TILE = 512
FFN_TILE = 512
SAMPLE_TILE = 64
FLASH_TILE = 256
FLASH_HEADS = 4
VT_ROWS = 80
LOG2E = 1.4426950408889634
VMEM_LIMIT = 56 * 1024 * 1024


def _cparams(sem):
    return pltpu.CompilerParams(dimension_semantics=sem, vmem_limit_bytes=VMEM_LIMIT)


def _dot(a, b):
    return jnp.dot(a, b, preferred_element_type=F32)


def _dot_t(a, b):
    return lax.dot_general(a, b, (((1,), (1,)), ((), ())), preferred_element_type=F32)


def _rms(x, g):
    return x * lax.rsqrt(jnp.mean(x * x, axis=-1, keepdims=True) + EPS) * g


def _modulate(x, g, shift, scale):
    return _rms(x, g) * (1.0 + scale) + shift


def _silu(x):
    return x * (1.0 / (1.0 + jnp.exp(-x)))


def _resident(shape):
    zeros = (0,) * len(shape)
    return pl.BlockSpec(shape, lambda *_: zeros, pipeline_mode=pl.Buffered(1))


def _ld(ref):
    v = ref[...]
    return v if v.ndim == 2 else v.reshape(-1, v.shape[-1])


def _st(ref, val):
    ref[...] = val.reshape(ref.shape).astype(ref.dtype)


def _mod(mod_ref, k, rows):
    m = mod_ref[k]
    if m.shape[0] in (1, rows):
        return m
    return jnp.concatenate([m] * (rows // m.shape[0]), axis=0)


class _Rows:
    def __init__(self, x, seq=None, tile=PROMPT_TILE):
        self.sample = x.ndim == 3
        if self.sample:
            self.steps, self.nb = x.shape[0], x.shape[1]
            self.nbt = min(SAMPLE_TILE, self.nb)
            self.grid = self.nb // self.nbt
            self.tile_rows = self.steps * self.nbt
        else:
            self.seq = seq
            self.tm = min(tile, seq)
            self.grid = x.shape[0] // self.tm
            self.tps = seq // self.tm
            self.tile_rows = self.tm

    def spec(self, w):
        if self.sample:
            return pl.BlockSpec((self.steps, self.nbt, w), lambda i: (0, i, 0))
        return pl.BlockSpec((self.tm, w), lambda i: (i, 0))

    def shape(self, x, w, dtype):
        return jax.ShapeDtypeStruct(x.shape[:-1] + (w,), dtype)

    def mod_spec(self):
        if self.sample:
            return pl.BlockSpec((3, self.nbt, D_MODEL), lambda i: (0, i, 0))
        tps = self.tps
        return pl.BlockSpec((3, None, 1, D_MODEL), lambda i: (0, i // tps, 0, 0))

    def table_spec(self, w):
        if self.sample:
            return _resident((self.tile_rows, w))
        tps = self.tps
        return pl.BlockSpec((self.tm, w), lambda i: (i % tps, 0))


def _adaln_kernel(c_ref, w_ref, b_ref, o_ref):
    a = _silu(c_ref[...]).astype(BF16)
    o_ref[...] = _dot(a, w_ref[...].astype(BF16)) + b_ref[...]


def _adaln(c_all, w, b):
    depth, d, n = w.shape
    rows = c_all.shape[0]
    return pl.pallas_call(
        _adaln_kernel,
        grid=(depth, n // d),
        in_specs=[
            pl.BlockSpec((rows, d), lambda i, j: (0, 0)),
            pl.BlockSpec((None, d, d), lambda i, j: (i, 0, j)),
            pl.BlockSpec((None, 1, d), lambda i, j: (i, 0, j)),
        ],
        out_specs=pl.BlockSpec((None, None, rows, d), lambda i, j: (i, j, 0, 0)),
        out_shape=jax.ShapeDtypeStruct((depth, n // d, rows, d), F32),
        compiler_params=_cparams(("parallel", "parallel")),
        name="adaln",
    )(c_all, w, b.reshape(depth, 1, n))


def _proj_common(x_ref, mod_ref, npre_ref, cosk, sink, wdq_ref, qn_ref, wkv_ref, kvn_ref, ckv_ref, kr_ref):
    x = _ld(x_ref)
    rows = x.shape[0]
    h = _modulate(x, npre_ref[...], _mod(mod_ref, 0, rows), _mod(mod_ref, 1, rows)).astype(BF16)
    cq = _rms(_dot(h, wdq_ref[...]), qn_ref[...]).astype(BF16)
    kv = _dot(h, wkv_ref[...])
    ckv = _rms(kv[:, :KV_LORA], kvn_ref[...])
    krp = kv[:, KV_LORA:KV_LORA + HEAD_PAD] * cosk + kv[:, KV_LORA + HEAD_PAD:] * sink
    _st(ckv_ref, ckv)
    _st(kr_ref, krp[:, :QK_ROPE])
    return cq, ckv, krp


def _proj_prompt_kernel(x_ref, mod_ref, npre_ref, cos_ref, sina_ref, sinb_ref, wdq_ref, qn_ref, wqa_ref,
                        wkv_ref, kvn_ref, wk_ref, wvt_ref, vbias_ref,
                        q_ref, k_ref, vt_ref, ckv_ref, kr_ref):
    cos = cos_ref[...]
    sin_a = sina_ref[...]
    sin_b = sinb_ref[...]
    cq, ckv, krp = _proj_common(x_ref, mod_ref, npre_ref, cos, sin_a + sin_b, wdq_ref, qn_ref, wkv_ref, kvn_ref,
                                ckv_ref, kr_ref)
    qa = _dot(cq, wqa_ref[...])
    half = QK_ROPE // 2
    for hd in range(N_HEADS):
        x = qa[:, hd * HEAD_PAD:(hd + 1) * HEAD_PAD]
        q = x * cos + pltpu.roll(x, HEAD_PAD - half, 1) * sin_a + pltpu.roll(x, half, 1) * sin_b
        q_ref[:, hd * HEAD_PAD:(hd + 1) * HEAD_PAD] = q.astype(q_ref.dtype)
    ckv_b = ckv.astype(BF16)
    kcat = jnp.concatenate([ckv_b, krp.astype(BF16)], axis=-1)
    k_ref[...] = _dot(kcat, wk_ref[...]).astype(k_ref.dtype)
    vt = (_dot_t(wvt_ref[...], ckv_b) + vbias_ref[...]).astype(vt_ref.dtype)
    tk = vt_ref.shape[-1]
    for c in range(vt_ref.shape[0]):
        vt_ref[c] = vt[:, c * tk:(c + 1) * tk]


def _proj_prompt(x2d, mod, npre, cos_t, sin_a, sin_b, wdq, qn, wqa, wkv, kvn, wk, wvt, vbias, *, seq):
    r = _Rows(x2d, seq)
    hq = N_HEADS * HEAD_PAD
    tk = min(FLASH_TILE, seq)
    kpt = r.tm // tk
    nbp = x2d.shape[0] // seq
    vt_rows = N_HEADS * VT_ROWS
    return pl.pallas_call(
        _proj_prompt_kernel,
        grid=(r.grid,),
        in_specs=[
            r.spec(D_MODEL), r.mod_spec(), _resident((1, D_MODEL)),
            r.table_spec(HEAD_PAD), r.table_spec(HEAD_PAD), r.table_spec(HEAD_PAD),
            _resident(wdq.shape), _resident(qn.shape), _resident(wqa.shape),
            _resident(wkv.shape), _resident(kvn.shape), _resident(wk.shape), _resident(wvt.shape),
            _resident(vbias.shape),
        ],
        out_specs=[r.spec(hq), r.spec(hq),
                   pl.BlockSpec((None, kpt, vt_rows, tk), lambda i: (i // r.tps, i % r.tps, 0, 0)),
                   r.spec(KV_LORA), r.spec(QK_ROPE)],
        out_shape=[r.shape(x2d, hq, BF16), r.shape(x2d, hq, BF16),
                   jax.ShapeDtypeStruct((nbp, seq // tk, vt_rows, tk), BF16),
                   r.shape(x2d, KV_LORA, F32), r.shape(x2d, QK_ROPE, F32)],
        compiler_params=_cparams(("parallel",)),
        name="mla_proj_prompt",
    )(x2d, mod, npre, cos_t, sin_a, sin_b, wdq, qn, wqa, wkv, kvn, wk, wvt, vbias)


def _proj_sample_kernel(x_ref, mod_ref, npre_ref, cosq_ref, sinq_ref, cosk_ref, sink_ref,
                        wdq_ref, qn_ref, wqn_ref, wqa_ref, wqb_ref, wkv_ref, kvn_ref, wukt_ref,
                        ql_ref, qr_ref, ckv_ref, kr_ref):
    cq, _, _ = _proj_common(x_ref, mod_ref, npre_ref, cosk_ref[...], sink_ref[...], wdq_ref, qn_ref, wkv_ref,
                            kvn_ref, ckv_ref, kr_ref)
    q_nope = _dot(cq, wqn_ref[...]).astype(BF16)
    for hd in range(N_HEADS):
        ql = _dot(q_nope[:, hd * QK_NOPE:(hd + 1) * QK_NOPE], wukt_ref[hd])
        ql_ref[:, :, hd * KV_LORA:(hd + 1) * KV_LORA] = ql.reshape(ql_ref.shape[:2] + (KV_LORA,)).astype(ql_ref.dtype)
    _st(qr_ref, _dot(cq, wqa_ref[...]) * cosq_ref[...] + _dot(cq, wqb_ref[...]) * sinq_ref[...])


def _proj_sample(x3d, mod, npre, cosq, sinq, cosk, sink, wdq, qn, wqn, wqa, wqb, wkv, kvn, wukt):
    r = _Rows(x3d)
    return pl.pallas_call(
        _proj_sample_kernel,
        grid=(r.grid,),
        in_specs=[
            r.spec(D_MODEL), r.mod_spec(), _resident((1, D_MODEL)),
            r.table_spec(N_HEADS * QK_ROPE), r.table_spec(N_HEADS * QK_ROPE),
            r.table_spec(HEAD_PAD), r.table_spec(HEAD_PAD),
            _resident(wdq.shape), _resident(qn.shape), _resident(wqn.shape), _resident(wqa.shape),
            _resident(wqb.shape), _resident(wkv.shape), _resident(kvn.shape), _resident(wukt.shape),
        ],
        out_specs=[r.spec(N_HEADS * KV_LORA), r.spec(N_HEADS * QK_ROPE), r.spec(KV_LORA), r.spec(QK_ROPE)],
        out_shape=[r.shape(x3d, N_HEADS * KV_LORA, BF16), r.shape(x3d, N_HEADS * QK_ROPE, BF16),
                   r.shape(x3d, KV_LORA, F32), r.shape(x3d, QK_ROPE, F32)],
        compiler_params=_cparams(("parallel",)),
        name="mla_proj_sample",
    )(x3d, mod, npre, cosq, sinq, cosk, sink, wdq, qn, wqn, wqa, wqb, wkv, kvn, wukt)


def _flash_kernel(q_ref, k_ref, vt_ref, o_ref, sa_ref, sb_ref, m_ref, acc_ref, *, tq, tk, heads):
    qi = pl.program_id(2)
    below = lax.broadcasted_iota(jnp.int32, (tq, tq), 0) <= lax.broadcasted_iota(jnp.int32, (tq, tq), 1)
    qs = [q_ref[:, h * HEAD_PAD:(h + 1) * HEAD_PAD] for h in range(heads)]

    def scores(j, s_ref):
        k0 = pl.multiple_of(j * tq, tq)
        for h in range(heads):
            s_ref[h] = _dot_t(k_ref[pl.ds(k0, tq), h * HEAD_PAD:(h + 1) * HEAD_PAD], qs[h])

    def consume(j, s_ref, diagonal):
        pts, ms, alphas = [], [], []
        for h in range(heads):
            st = s_ref[h]
            if diagonal:
                st = jnp.where(below, st, -jnp.inf)
            m_old = m_ref[h]
            m_new = jnp.maximum(m_old, jnp.max(st, axis=0, keepdims=True))
            alphas.append(jnp.exp2(m_old - m_new))
            pts.append(jnp.exp2(st - m_new).astype(BF16))
            ms.append(m_new)
        for h in range(heads):
            rows = slice(h * VT_ROWS, (h + 1) * VT_ROWS)
            pv = sum(_dot(vt_ref[(tq // tk) * j + c, rows, :], pts[h][c * tk:(c + 1) * tk])
                     for c in range(tq // tk))
            acc_ref[h] = alphas[h] * acc_ref[h] + pv
            m_ref[h] = ms[h]

    m_ref[...] = jnp.full(m_ref.shape, -jnp.inf, F32)
    acc_ref[...] = jnp.zeros(acc_ref.shape, F32)
    scores(0, sa_ref)

    def two_spans(t, c):
        scores(2 * t + 1, sb_ref)
        consume(2 * t, sa_ref, False)
        scores(2 * t + 2, sa_ref)
        consume(2 * t + 1, sb_ref, False)
        return c

    lax.fori_loop(0, qi // 2, two_spans, 0)

    @pl.when(qi % 2 == 0)
    def _():
        consume(qi, sa_ref, True)

    @pl.when(qi % 2 == 1)
    def _():
        scores(qi, sb_ref)
        consume(qi - 1, sa_ref, False)
        consume(qi, sb_ref, True)

    for h in range(heads):
        acc = acc_ref[h]
        o_ref[h * V_HEAD:(h + 1) * V_HEAD, :] = (acc[:V_HEAD] / acc[V_HEAD:V_HEAD + 1]).astype(o_ref.dtype)


def _flash(q, k, vt):
    nb, seq, _ = q.shape
    tk = vt.shape[-1]
    tq = 2 * tk
    heads = FLASH_HEADS
    assert seq % tq == 0 and N_HEADS % heads == 0
    return pl.pallas_call(
        functools.partial(_flash_kernel, tq=tq, tk=tk, heads=heads),
        grid=(nb, N_HEADS // heads, seq // tq),
        in_specs=[
            pl.BlockSpec((None, tq, heads * HEAD_PAD), lambda b, g, i: (b, i, g)),
            pl.BlockSpec((None, seq, heads * HEAD_PAD), lambda b, g, i: (b, 0, g)),
            pl.BlockSpec((None, seq // tk, heads * VT_ROWS, tk), lambda b, g, i: (b, 0, g, 0)),
        ],
        out_specs=pl.BlockSpec((None, heads * V_HEAD, tq), lambda b, g, i: (b, g, i)),
        out_shape=jax.ShapeDtypeStruct((nb, N_HEADS * V_HEAD, seq), BF16),
        scratch_shapes=[pltpu.VMEM((heads, tq, tq), F32), pltpu.VMEM((heads, tq, tq), F32),
                        pltpu.VMEM((heads, 1, tq), F32), pltpu.VMEM((heads, VT_ROWS, tq), F32)],
        compiler_params=_cparams(("parallel", "parallel", "arbitrary")),
        name="flash_prompt",
    )(q, k, vt)


def _sattn_kernel(pt_ref, ql_ref, qr_ref, cn_ref, rn_ref, ckv_hbm, krt_hbm, o_ref, kvbuf, krbuf, sem,
                  s_a, s_b, kv_a, kv_b, *, layer, n_pages, chunk_pages):
    b = pl.program_id(0)
    nb = pl.num_programs(0)
    slot = b % 2

    def page_copies(bb, sl, p):
        page = pt_ref[bb * n_pages + p]
        return (pltpu.make_async_copy(ckv_hbm.at[layer, page], kvbuf.at[sl, p], sem.at[0, sl]),
                pltpu.make_async_copy(krt_hbm.at[layer, page], krbuf.at[sl, p], sem.at[1, sl]))

    def start_fetch(bb, sl):
        for p in range(n_pages):
            for cp in page_copies(bb, sl, p):
                cp.start()

    def wait_fetch(bb, sl):
        for p in range(n_pages):
            for cp in page_copies(bb, sl, p):
                cp.wait()

    @pl.when(b == 0)
    def _():
        start_fetch(0, 0)

    @pl.when(b + 1 < nb)
    def _():
        start_fetch(b + 1, 1 - slot)

    wait_fetch(b, slot)

    ql = ql_ref[...].reshape(-1, KV_LORA)
    qr = qr_ref[...].reshape(-1, QK_ROPE)
    rows = ql.shape[0]
    ck = chunk_pages * PAGE_SIZE

    s_bufs = (s_a, s_b)
    kv_bufs = (kv_a, kv_b)

    def scores(c):
        p0 = c * chunk_pages
        kv = kvbuf[slot, p0:p0 + chunk_pages].reshape(ck, KV_LORA).astype(BF16)
        kv_bufs[c % 2][...] = kv
        krt = jnp.concatenate([krbuf[slot, p0 + i] for i in range(chunk_pages)], axis=1).astype(BF16)
        s_bufs[c % 2][...] = _dot_t(ql, kv) + _dot(qr, krt)

    def softmax_step(s, values, carry):
        m, l, acc = carry
        m_new = jnp.maximum(m, jnp.max(s, axis=-1, keepdims=True))
        alpha = jnp.exp(m - m_new)
        p = jnp.exp(s - m_new)
        l = alpha * l + jnp.sum(p, axis=-1, keepdims=True)
        return m_new, l, alpha * acc + _dot(p.astype(BF16), values)

    carry = (jnp.full((rows, 1), -jnp.inf, F32), jnp.zeros((rows, 1), F32), jnp.zeros((rows, KV_LORA), F32))
    n_chunks = n_pages // chunk_pages
    scores(0)
    for c in range(n_chunks):
        if c + 1 < n_chunks:
            scores(c + 1)
        carry = softmax_step(s_bufs[c % 2][...], kv_bufs[c % 2][...], carry)

    cn = cn_ref[...].astype(BF16)
    rn = rn_ref[...].astype(BF16)
    s = _dot_t(ql, cn) + _dot_t(qr, rn)
    qstep = lax.broadcasted_iota(jnp.int32, s.shape, 0) // N_HEADS
    kstep = lax.broadcasted_iota(jnp.int32, s.shape, 1)
    s = jnp.where(kstep <= qstep, s, -jnp.inf)
    _, l, acc = softmax_step(s, cn, carry)
    o_ref[...] = (acc / l).astype(o_ref.dtype).reshape(o_ref.shape)


def _sattn(page_table, ql, qr, cn, rn, cache_kv, cache_krt, *, layer, chunk_pages):
    steps, nb = ql.shape[0], ql.shape[1]
    rows = steps * N_HEADS
    n_pages = page_table.shape[1]
    n_new = cn.shape[1]
    ck = chunk_pages * PAGE_SIZE
    qspec = lambda w: pl.BlockSpec((steps, None, N_HEADS, w), lambda b, pt: (0, b, 0, 0))
    grid_spec = pltpu.PrefetchScalarGridSpec(
        num_scalar_prefetch=1,
        grid=(nb,),
        in_specs=[
            qspec(KV_LORA),
            qspec(QK_ROPE),
            pl.BlockSpec((None, n_new, KV_LORA), lambda b, pt: (b, 0, 0)),
            pl.BlockSpec((None, n_new, QK_ROPE), lambda b, pt: (b, 0, 0)),
            pl.BlockSpec(memory_space=pl.ANY),
            pl.BlockSpec(memory_space=pl.ANY),
        ],
        out_specs=qspec(KV_LORA),
        scratch_shapes=[
            pltpu.VMEM((2, n_pages, PAGE_SIZE, KV_LORA), F32),
            pltpu.VMEM((2, n_pages, QK_ROPE, PAGE_SIZE), F32),
            pltpu.SemaphoreType.DMA((2, 2)),
            pltpu.VMEM((rows, ck), F32), pltpu.VMEM((rows, ck), F32),
            pltpu.VMEM((ck, KV_LORA), BF16), pltpu.VMEM((ck, KV_LORA), BF16),
        ],
    )
    return pl.pallas_call(
        functools.partial(_sattn_kernel, layer=layer, n_pages=n_pages, chunk_pages=chunk_pages),
        grid_spec=grid_spec,
        out_shape=jax.ShapeDtypeStruct((steps, nb, N_HEADS, KV_LORA), BF16),
        compiler_params=_cparams(("arbitrary",)),
        name="sample_attend",
    )(page_table.reshape(-1), ql, qr, cn, rn, cache_kv, cache_krt)


def _mla_out_kernel(*refs, absorbed):
    if absorbed:
        o_ref, x_ref, mod_ref, npost_ref, wuv_ref, wo_ref, y_ref = refs
        o_lat = _ld(o_ref)
        o = jnp.concatenate([_dot(o_lat[:, hd * KV_LORA:(hd + 1) * KV_LORA], wuv_ref[hd])
                             for hd in range(N_HEADS)], axis=-1).astype(BF16)
        f = _dot(o, wo_ref[...])
    else:
        o_ref, x_ref, mod_ref, npost_ref, wo_ref, y_ref = refs
        f = lax.dot_general(o_ref[...], wo_ref[...], (((0,), (0,)), ((), ())), preferred_element_type=F32)
    x = _ld(x_ref)
    _st(y_ref, x + _mod(mod_ref, 2, x.shape[0]) * _rms(f, npost_ref[...]))


def _mla_out(o, x, mod, npost, wo, wuv=None, *, seq=None):
    r = _Rows(x, seq)
    if wuv is not None:
        o_spec = r.spec(o.shape[-1])
    else:
        o_spec = pl.BlockSpec((None, o.shape[1], r.tm), lambda i: (i // r.tps, 0, i % r.tps))
    in_specs = [o_spec, r.spec(D_MODEL), r.mod_spec(), _resident((1, D_MODEL))]
    args = [o, x, mod, npost]
    if wuv is not None:
        in_specs.append(_resident(wuv.shape))
        args.append(wuv)
    in_specs.append(_resident(wo.shape))
    args.append(wo)
    return pl.pallas_call(
        functools.partial(_mla_out_kernel, absorbed=wuv is not None),
        grid=(r.grid,),
        in_specs=in_specs,
        out_specs=r.spec(D_MODEL),
        out_shape=r.shape(x, D_MODEL, F32),
        compiler_params=_cparams(("parallel",)),
        name="mla_out_sample" if wuv is not None else "mla_out_prompt",
    )(*args)


def _pool_finish(window_means, h, x, mod_ref, npost_ref, wg_ref, ls_ref, y_ref):
    outs = []
    for g in range(len(POOL_WINDOWS)):
        cols = slice(g * GROUP_W, (g + 1) * GROUP_W)
        pooled = (window_means[g] - h[:, cols]).astype(BF16)
        outs.append(_dot(pooled, wg_ref[g]))
    o = jnp.concatenate(outs, axis=-1) * ls_ref[...]
    _st(y_ref, x + _mod(mod_ref, 2, x.shape[0]) * _rms(o, npost_ref[...]))


def _pool_prompt_kernel(x_ref, xh_ref, mod_ref, npre_ref, npost_ref, wg_ref, ls_ref, y_ref, st_ref, hext_ref,
                        *, tm, tps):
    t_in_seq = pl.program_id(0) % tps
    keep = jnp.where(t_in_seq == 0, 0.0, 1.0)
    x = x_ref[...]
    h = _modulate(x, npre_ref[...], mod_ref[0], mod_ref[1])
    hh = _modulate(xh_ref[...], npre_ref[...], mod_ref[0], mod_ref[1]) * keep
    hext_ref[0:POOL_HALO, :] = hh
    hext_ref[POOL_HALO:, :] = h
    st_ref[...] = h[tm - POOL_HALO:, :]
    pos = t_in_seq * tm + lax.broadcasted_iota(jnp.int32, (tm, 1), 0)
    means = []
    for g, w in enumerate(POOL_WINDOWS):
        cols = slice(g * GROUP_W, (g + 1) * GROUP_W)
        s = h[:, cols]
        for j in range(1, w):
            s = s + hext_ref[POOL_HALO - j:POOL_HALO - j + tm, cols]
        means.append(s / jnp.minimum(pos + 1, w).astype(F32))
    _pool_finish(means, h, x, mod_ref, npost_ref, wg_ref, ls_ref, y_ref)


def _pool_prompt(x2d, mod, npre, npost, wg, ls, *, seq):
    r = _Rows(x2d, seq)
    hb = r.tm // POOL_HALO
    return pl.pallas_call(
        functools.partial(_pool_prompt_kernel, tm=r.tm, tps=r.tps),
        grid=(r.grid,),
        in_specs=[
            r.spec(D_MODEL),
            pl.BlockSpec((POOL_HALO, D_MODEL), lambda i: (jnp.maximum(i * hb - 1, 0), 0)),
            r.mod_spec(),
            _resident((1, D_MODEL)), _resident((1, D_MODEL)), _resident(wg.shape), _resident((1, D_MODEL)),
        ],
        out_specs=[r.spec(D_MODEL), pl.BlockSpec((None, POOL_HALO, D_MODEL), lambda i: (i // r.tps, 0, 0))],
        out_shape=[r.shape(x2d, D_MODEL, F32),
                   jax.ShapeDtypeStruct((x2d.shape[0] // seq, POOL_HALO, D_MODEL), F32)],
        scratch_shapes=[pltpu.VMEM((r.tm + POOL_HALO, D_MODEL), F32)],
        compiler_params=_cparams(("arbitrary",)),
        name="pool_prompt",
    )(x2d, x2d, mod, npre, npost, wg, ls)


def _pool_sample_kernel(x_ref, st_ref, mod_ref, npre_ref, npost_ref, wg_ref, ls_ref, y_ref, so_ref, *, past):
    steps, nbt, _ = x_ref.shape
    x = _ld(x_ref)
    h = _modulate(x, npre_ref[...], _mod(mod_ref, 0, x.shape[0]), _mod(mod_ref, 1, x.shape[0]))

    def slab(k, cols):
        if k < POOL_BUF:
            return st_ref[k, :, cols]
        return h[(k - POOL_BUF) * nbt:(k - POOL_BUF + 1) * nbt, cols]

    for k in range(POOL_BUF):
        so_ref[k] = slab(k + steps, slice(None))
    means = []
    for g, w in enumerate(POOL_WINDOWS):
        cols = slice(g * GROUP_W, (g + 1) * GROUP_W)
        parts = []
        for s_ in range(steps):
            acc = slab(POOL_BUF + s_, cols)
            for j in range(1, w):
                acc = acc + slab(POOL_BUF + s_ - j, cols)
            parts.append(acc / float(min(past + s_ + 1, w)))
        means.append(jnp.concatenate(parts, axis=0))
    _pool_finish(means, h, x, mod_ref, npost_ref, wg_ref, ls_ref, y_ref)


def _pool_sample(x3d, st3d, mod, npre, npost, wg, ls, *, past):
    r = _Rows(x3d)
    st_spec = pl.BlockSpec((POOL_BUF, r.nbt, D_MODEL), lambda i: (0, i, 0))
    return pl.pallas_call(
        functools.partial(_pool_sample_kernel, past=past),
        grid=(r.grid,),
        in_specs=[r.spec(D_MODEL), st_spec, r.mod_spec(),
                  _resident((1, D_MODEL)), _resident((1, D_MODEL)), _resident(wg.shape), _resident((1, D_MODEL))],
        out_specs=[r.spec(D_MODEL), st_spec],
        out_shape=[r.shape(x3d, D_MODEL, F32), jax.ShapeDtypeStruct(st3d.shape, F32)],
        compiler_params=_cparams(("parallel",)),
        name="pool_sample",
    )(x3d, st3d, mod, npre, npost, wg, ls)


def _ffn_chunks(h, bufs, put_fn, conv_fn, cw_ref, cb_ref, wup_ref, wdn_ref):
    n = D_FF // FF_CHUNK
    cols = lambda c, off: slice(off + c * FF_CHUNK, off + (c + 1) * FF_CHUNK)

    def up(c):
        for k, off in enumerate((0, D_FF)):
            put_fn(bufs[c % 2], k, _dot(h, wup_ref[:, cols(c, off)]))

    up(0)
    acc = None
    for c in range(n):
        halves = []
        for k, off in enumerate((0, D_FF)):
            u, r1, r2 = conv_fn(bufs[c % 2], k, cols(c, off))
            cw = cw_ref[:, cols(c, off)]
            halves.append(cb_ref[:, cols(c, off)] + r2 * cw[0:1] + r1 * cw[1:2] + u * cw[2:3])
        if c + 1 < n:
            up(c + 1)
        gate = (_silu(halves[0]) * halves[1]).astype(BF16)
        part = _dot(gate, wdn_ref[c * FF_CHUNK:(c + 1) * FF_CHUNK, :])
        acc = part if acc is None else acc + part
    return acc


def _ffn_prompt_kernel(x_ref, xh_ref, mod_ref, npre_ref, npost_ref, wup_ref, cw_ref, cb_ref, wdn_ref,
                       y_ref, st_ref, buf_a, buf_b, *, tm, tps):
    keep = jnp.where(pl.program_id(0) % tps == 0, 0.0, 1.0)
    x = x_ref[...]
    xe = jnp.concatenate([xh_ref[...], x], axis=0)
    he = _modulate(xe, npre_ref[...], mod_ref[0], mod_ref[1]).astype(BF16)
    hl = CONV_HALO

    def put_fn(buf, k, ue):
        buf[k, :hl] = ue[:hl] * keep
        buf[k, hl:] = ue[hl:]

    def conv_fn(buf, k, cols):
        st_ref[:, cols] = buf[k, hl + tm - STATE_ROWS:hl + tm]
        return buf[k, hl:hl + tm], buf[k, hl - 1:hl - 1 + tm], buf[k, hl - 2:hl - 2 + tm]

    f = _ffn_chunks(he, (buf_a, buf_b), put_fn, conv_fn, cw_ref, cb_ref, wup_ref, wdn_ref)
    y_ref[...] = x + mod_ref[2] * _rms(f, npost_ref[...])


def _ffn_prompt(x2d, mod, npre, npost, wup, cw, cb, wdn, *, layer, seq):
    r = _Rows(x2d, seq, FFN_TILE)
    hb = r.tm // CONV_HALO
    lyr = lambda a: pl.BlockSpec((None,) + a.shape[1:], lambda i: (layer,) + (0,) * (a.ndim - 1),
                                 pipeline_mode=pl.Buffered(1))
    return pl.pallas_call(
        functools.partial(_ffn_prompt_kernel, tm=r.tm, tps=r.tps),
        grid=(r.grid,),
        in_specs=[
            r.spec(D_MODEL),
            pl.BlockSpec((CONV_HALO, D_MODEL), lambda i: (jnp.maximum(i * hb - 1, 0), 0)),
            r.mod_spec(),
            _resident((1, D_MODEL)), _resident((1, D_MODEL)),
            lyr(wup), _resident(cw.shape), _resident(cb.shape), lyr(wdn),
        ],
        out_specs=[r.spec(D_MODEL), pl.BlockSpec((None, STATE_ROWS, 2 * D_FF), lambda i: (i // r.tps, 0, 0))],
        out_shape=[r.shape(x2d, D_MODEL, F32),
                   jax.ShapeDtypeStruct((x2d.shape[0] // seq, STATE_ROWS, 2 * D_FF), F32)],
        scratch_shapes=[pltpu.VMEM((2, r.tm + CONV_HALO, FF_CHUNK), F32)] * 2,
        compiler_params=_cparams(("arbitrary",)),
        name="ffn_prompt",
    )(x2d, x2d, mod, npre, npost, wup, cw, cb, wdn)


def _ffn_sample_kernel(x_ref, st_ref, mod_ref, npre_ref, npost_ref, wup_ref, cw_ref, cb_ref, wdn_ref,
                       y_ref, so_ref, buf_a, buf_b):
    steps, nbt, _ = x_ref.shape
    x = _ld(x_ref)
    rows = x.shape[0]
    h = _modulate(x, npre_ref[...], _mod(mod_ref, 0, rows), _mod(mod_ref, 1, rows)).astype(BF16)

    def put_fn(buf, k, u):
        buf[k] = u

    def conv_fn(buf, k, cols):
        s0 = st_ref[0, :, cols]
        s1 = st_ref[1, :, cols]
        so_ref[0, :, cols] = buf[k, rows - 2 * nbt:rows - nbt]
        so_ref[1, :, cols] = buf[k, rows - nbt:rows]
        return (buf[k], jnp.concatenate([s1, buf[k, :rows - nbt]], axis=0),
                jnp.concatenate([s0, s1, buf[k, :rows - 2 * nbt]], axis=0))

    f = _ffn_chunks(h, (buf_a, buf_b), put_fn, conv_fn, cw_ref, cb_ref, wup_ref, wdn_ref)
    _st(y_ref, x + _mod(mod_ref, 2, rows) * _rms(f, npost_ref[...]))


def _ffn_sample(x3d, st3d, mod, npre, npost, wup, cw, cb, wdn, *, layer):
    r = _Rows(x3d)
    st_spec = pl.BlockSpec((CONV_BUF, r.nbt, 2 * D_FF), lambda i: (0, i, 0))
    lyr = lambda a: pl.BlockSpec((None,) + a.shape[1:], lambda i: (layer,) + (0,) * (a.ndim - 1),
                                 pipeline_mode=pl.Buffered(1))
    return pl.pallas_call(
        _ffn_sample_kernel,
        grid=(r.grid,),
        in_specs=[r.spec(D_MODEL), st_spec, r.mod_spec(),
                  _resident((1, D_MODEL)), _resident((1, D_MODEL)),
                  lyr(wup), _resident(cw.shape), _resident(cb.shape), lyr(wdn)],
        out_specs=[r.spec(D_MODEL), st_spec],
        out_shape=[r.shape(x3d, D_MODEL, F32), jax.ShapeDtypeStruct(st3d.shape, F32)],
        scratch_shapes=[pltpu.VMEM((2, r.tile_rows, FF_CHUNK), F32)] * 2,
        compiler_params=_cparams(("parallel",)),
        name="ffn_sample",
    )(x3d, st3d, mod, npre, npost, wup, cw, cb, wdn)


def _rope_angles(pos):
    half = QK_ROPE // 2
    inv = ROPE_THETA ** (-jnp.arange(half, dtype=F32) / half)
    ang = pos.astype(F32)[:, None] * inv[None, :]
    return jnp.cos(ang), jnp.sin(ang)


def _rope_tables(pos, width):
    cos, sin = _rope_angles(pos)
    n = pos.shape[0]
    cos_t = jnp.concatenate([cos, cos, jnp.ones((n, width - QK_ROPE), F32)], axis=-1)
    sin_t = jnp.concatenate([-sin, sin, jnp.zeros((n, width - QK_ROPE), F32)], axis=-1)
    return cos_t, sin_t


def _swap_halves(w):
    half = QK_ROPE // 2
    return jnp.concatenate([w[..., half:], w[..., :half]], axis=-1)


def _mla_weights(w_dq, w_uq, w_dkv, w_uk, w_uv, w_o):
    wq3 = w_uq.reshape(Q_LORA, N_HEADS, QK_NOPE + QK_ROPE) * ATTN_SCALE
    nope, ropew = wq3[..., :QK_NOPE], wq3[..., QK_NOPE:]
    ropesw = _swap_halves(ropew)
    zpad = jnp.zeros((Q_LORA, N_HEADS, HEAD_PAD - QK_NOPE - QK_ROPE), F32)
    wqa_p = jnp.concatenate([ropew, nope, zpad], axis=-1).reshape(Q_LORA, N_HEADS * HEAD_PAD)
    kr_w = w_dkv[:, KV_LORA:]
    lane_pad = jnp.zeros((D_MODEL, HEAD_PAD - QK_ROPE), F32)
    wkv = jnp.concatenate([w_dkv[:, :KV_LORA], kr_w, lane_pad, _swap_halves(kr_w), lane_pad], axis=-1)
    eye_r = jnp.eye(QK_ROPE, dtype=F32)
    top = jnp.concatenate([jnp.zeros((KV_LORA, N_HEADS, QK_ROPE), F32), w_uk,
                           jnp.zeros((KV_LORA, N_HEADS, HEAD_PAD - QK_NOPE - QK_ROPE), F32)], axis=-1)
    mid = jnp.concatenate([jnp.broadcast_to(eye_r[:, None, :], (QK_ROPE, N_HEADS, QK_ROPE)),
                           jnp.zeros((QK_ROPE, N_HEADS, HEAD_PAD - QK_ROPE), F32)], axis=-1)
    bot = jnp.zeros((HEAD_PAD - QK_ROPE, N_HEADS, HEAD_PAD), F32)
    wk = jnp.concatenate([top, mid, bot], axis=0).reshape(KV_LORA + HEAD_PAD, N_HEADS * HEAD_PAD)
    wvt = jnp.pad(jnp.transpose(w_uv, (1, 2, 0)), ((0, 0), (0, VT_ROWS - V_HEAD), (0, 0)))
    wvt = wvt.reshape(N_HEADS * VT_ROWS, KV_LORA)
    vbias = jnp.tile((jnp.arange(VT_ROWS) == V_HEAD).astype(F32), N_HEADS).reshape(N_HEADS * VT_ROWS, 1)
    wukt = jnp.transpose(w_uk, (1, 2, 0))
    wuv = jnp.transpose(w_uv, (1, 0, 2))
    b = lambda a: a.astype(BF16)
    return dict(
        wdq=b(w_dq), wqa_p=b(wqa_p * LOG2E), wkv=b(wkv), wk=b(wk), wvt=b(wvt), vbias=vbias,
        wqn=b(nope.reshape(Q_LORA, N_HEADS * QK_NOPE)),
        wqa_s=b(ropew.reshape(Q_LORA, N_HEADS * QK_ROPE)), wqb_s=b(ropesw.reshape(Q_LORA, N_HEADS * QK_ROPE)),
        wukt=b(wukt), wuv=b(wuv), wo=b(w_o))


def kernel(x_prompt, x_sample, cache_kv_latent, cache_k_rope, state_pool, state_conv, page_table, c_prompt, c_sample, ada_w, ada_b, norm_mix_pre, norm_mix_post, norm_ffn_pre, norm_ffn_post, mla_w_dq, mla_q_norm, mla_w_uq, mla_w_dkv, mla_kv_norm, mla_w_uk, mla_w_uv, mla_w_o, pool_w, pool_scale, ffn_w_up, ffn_conv_w, ffn_conv_b, ffn_w_down):
    nbp, seq, d = x_prompt.shape
    nbs, steps, _ = x_sample.shape
    depth = ada_w.shape[0]
    past = page_table.shape[1] * PAGE_SIZE
    row = lambda a: a.reshape(1, -1)
    tmajor = lambda a: jnp.transpose(a, (1, 0, 2))

    c_all = jnp.concatenate([c_prompt, c_sample], axis=0)
    c_all = jnp.pad(c_all, ((0, (-c_all.shape[0]) % 8), (0, 0)))
    mods = _adaln(c_all, ada_w, ada_b)

    def mods_prompt(i, k0):
        return mods[i, k0:k0 + 3, :nbp, None, :]

    def mods_sample(i, k0):
        return mods[i, k0:k0 + 3, nbp:nbp + nbs]

    wup_all = ffn_w_up.astype(BF16)
    wdn_all = ffn_w_down.astype(BF16)
    xp = x_prompt.reshape(nbp * seq, d)
    xs = tmajor(x_sample)
    nbt = min(SAMPLE_TILE, nbs)

    outs = dict(kv_p=[], kr_p=[], kv_s=[], kr_s=[], pool_p=[], pool_s=[], conv_p=[], conv_s=[])
    for i in range(depth):
        j = i // 2
        if i % 2 == 0:
            w = _mla_weights(mla_w_dq[j], mla_w_uq[j], mla_w_dkv[j], mla_w_uk[j], mla_w_uv[j], mla_w_o[j])
            cos_p, sin_p = _rope_tables(jnp.arange(seq), HEAD_PAD)
            first_half = jnp.arange(HEAD_PAD) < QK_ROPE // 2
            sin_a, sin_b = jnp.where(first_half, sin_p, 0.0), jnp.where(first_half, 0.0, sin_p)
            mp = mods_prompt(i, 0)
            q, k, vt, ckv, kr = _proj_prompt(
                xp, mp, row(norm_mix_pre[i]), cos_p, sin_a, sin_b, w['wdq'], row(mla_q_norm[j]), w['wqa_p'],
                w['wkv'], row(mla_kv_norm[j]), w['wk'], w['wvt'], w['vbias'], seq=seq)
            hq = N_HEADS * HEAD_PAD
            o_t = _flash(q.reshape(nbp, seq, hq), k.reshape(nbp, seq, hq), vt)
            xp = _mla_out(o_t, xp, mp, row(norm_mix_post[i]), w['wo'], seq=seq)
            outs['kv_p'].append(ckv.reshape(nbp, seq, KV_LORA))
            outs['kr_p'].append(kr.reshape(nbp, seq, QK_ROPE))
            cos_k, sin_k = _rope_tables(past + jnp.repeat(jnp.arange(steps), nbt), HEAD_PAD)
            cos_q = jnp.tile(cos_k[:, :QK_ROPE], (1, N_HEADS))
            sin_q = jnp.tile(sin_k[:, :QK_ROPE], (1, N_HEADS))
            ms = mods_sample(i, 0)
            ql, qr, ckv_s, kr_s = _proj_sample(
                xs, ms, row(norm_mix_pre[i]), cos_q, sin_q, cos_k, sin_k, w['wdq'], row(mla_q_norm[j]),
                w['wqn'], w['wqa_s'], w['wqb_s'], w['wkv'], row(mla_kv_norm[j]), w['wukt'])
            ckv_sb = tmajor(ckv_s)
            kr_sb = tmajor(kr_s)
            key_pad = ((0, 0), (0, NEW_KEY_ROWS - steps), (0, 0))
            o_lat = _sattn(page_table, ql.reshape(steps, nbs, N_HEADS, KV_LORA),
                           qr.reshape(steps, nbs, N_HEADS, QK_ROPE),
                           jnp.pad(ckv_sb, key_pad), jnp.pad(kr_sb, key_pad),
                           cache_kv_latent, jnp.swapaxes(cache_k_rope, 2, 3), layer=j, chunk_pages=16)
            xs = _mla_out(o_lat.reshape(steps, nbs, N_HEADS * KV_LORA), xs, ms, row(norm_mix_post[i]),
                          w['wo'], w['wuv'])
            outs['kv_s'].append(ckv_sb)
            outs['kr_s'].append(kr_sb)
        else:
            wg = pool_w[j].astype(BF16)
            xp, st_p = _pool_prompt(xp, mods_prompt(i, 0), row(norm_mix_pre[i]), row(norm_mix_post[i]), wg,
                                    row(pool_scale[j]), seq=seq)
            outs['pool_p'].append(st_p[:, POOL_HALO - POOL_BUF:])
            xs, st_s = _pool_sample(xs, tmajor(state_pool[j]), mods_sample(i, 0), row(norm_mix_pre[i]),
                                    row(norm_mix_post[i]), wg, row(pool_scale[j]), past=past)
            outs['pool_s'].append(tmajor(st_s))

        xp, cst_p = _ffn_prompt(xp, mods_prompt(i, 3), row(norm_ffn_pre[i]), row(norm_ffn_post[i]), wup_all,
                                ffn_conv_w[i], row(ffn_conv_b[i]), wdn_all, layer=i, seq=seq)
        outs['conv_p'].append(cst_p[:, STATE_ROWS - CONV_BUF:])
        xs, cst_s = _ffn_sample(xs, tmajor(state_conv[i]), mods_sample(i, 3), row(norm_ffn_pre[i]),
                                row(norm_ffn_post[i]), wup_all, ffn_conv_w[i], row(ffn_conv_b[i]), wdn_all,
                                layer=i)
        outs['conv_s'].append(tmajor(cst_s))

    st = lambda key: jnp.stack(outs[key])
    return (xp.reshape(nbp, seq, d), tmajor(xs), st('kv_p'), st('kr_p'), st('kv_s'), st('kr_s'),
            st('pool_p'), st('pool_s'), st('conv_p'), st('conv_s'))
```

```python
import functools

import jax
import jax.numpy as jnp
from jax import lax
from jax.experimental import pallas as pl
from jax.experimental.pallas import tpu as pltpu

F32 = jnp.float32
BF16 = jnp.bfloat16

D_MODEL = 1024
N_HEADS = 16
QK_NOPE = 64
QK_ROPE = 32
V_HEAD = 64
Q_LORA = 512
KV_LORA = 256
ROPE_THETA = 10000.0
ATTN_SCALE = (QK_NOPE + QK_ROPE) ** -0.5
PAGE_SIZE = 128
POOL_WINDOWS = (2, 4, 8, 16)
GROUP_W = D_MODEL // len(POOL_WINDOWS)
POOL_BUF = 15
D_FF = 2816
CONV_W = 3
CONV_BUF = CONV_W - 1
EPS = 1e-6

HEAD_PAD = 128
POOL_HALO = 16
CONV_HALO = 16
STATE_ROWS = 8
FF_CHUNK = 256
NEW_KEY_ROWS = 16
PROMPT_TILE = 512
FFN_TILE = 512
SAMPLE_TILE = 64
FLASH_TILE = 256
FLASH_HEADS = 4
VT_ROWS = 80
LOG2E = 1.4426950408889634
VMEM_LIMIT = 56 * 1024 * 1024


def _cparams(sem):
    return pltpu.CompilerParams(dimension_semantics=sem, vmem_limit_bytes=VMEM_LIMIT)


def _dot(a, b):
    return jnp.dot(a, b, preferred_element_type=F32)


def _dot_t(a, b):
    return lax.dot_general(a, b, (((1,), (1,)), ((), ())), preferred_element_type=F32)


def _rms(x, g):
    return x * lax.rsqrt(jnp.mean(x * x, axis=-1, keepdims=True) + EPS) * g


def _modulate(x, g, shift, scale):
    return _rms(x, g) * (1.0 + scale) + shift


def _silu(x):
    return x * (1.0 / (1.0 + jnp.exp(-x)))


def _resident(shape):
    zeros = (0,) * len(shape)
    return pl.BlockSpec(shape, lambda *_: zeros, pipeline_mode=pl.Buffered(1))


def _ld(ref):
    v = ref[...]
    return v if v.ndim == 2 else v.reshape(-1, v.shape[-1])


def _st(ref, val):
    ref[...] = val.reshape(ref.shape).astype(ref.dtype)


def _mod(mod_ref, k, rows):
    m = mod_ref[k]
    if m.shape[0] in (1, rows):
        return m
    return jnp.concatenate([m] * (rows // m.shape[0]), axis=0)


class _Rows:
    def __init__(self, x, seq=None, tile=PROMPT_TILE):
        self.sample = x.ndim == 3
        if self.sample:
            self.steps, self.nb = x.shape[0], x.shape[1]
            self.nbt = min(SAMPLE_TILE, self.nb)
            self.grid = self.nb // self.nbt
            self.tile_rows = self.steps * self.nbt
        else:
            self.seq = seq
            self.tm = min(tile, seq)
            self.grid = x.shape[0] // self.tm
            self.tps = seq // self.tm
            self.tile_rows = self.tm

    def spec(self, w):
        if self.sample:
            return pl.BlockSpec((self.steps, self.nbt, w), lambda i: (0, i, 0))
        return pl.BlockSpec((self.tm, w), lambda i: (i, 0))

    def shape(self, x, w, dtype):
        return jax.ShapeDtypeStruct(x.shape[:-1] + (w,), dtype)

    def mod_spec(self):
        if self.sample:
            return pl.BlockSpec((3, self.nbt, D_MODEL), lambda i: (0, i, 0))
        tps = self.tps
        return pl.BlockSpec((3, None, 1, D_MODEL), lambda i: (0, i // tps, 0, 0))

    def table_spec(self, w):
        if self.sample:
            return _resident((self.tile_rows, w))
        tps = self.tps
        return pl.BlockSpec((self.tm, w), lambda i: (i % tps, 0))


def _adaln_kernel(c_ref, w_ref, b_ref, o_ref):
    a = _silu(c_ref[...]).astype(BF16)
    o_ref[...] = _dot(a, w_ref[...].astype(BF16)) + b_ref[...]


def _adaln(c_all, w, b):
    depth, d, n = w.shape
    rows = c_all.shape[0]
    return pl.pallas_call(
        _adaln_kernel,
        grid=(depth, n // d),
        in_specs=[
            pl.BlockSpec((rows, d), lambda i, j: (0, 0)),
            pl.BlockSpec((None, d, d), lambda i, j: (i, 0, j)),
            pl.BlockSpec((None, 1, d), lambda i, j: (i, 0, j)),
        ],
        out_specs=pl.BlockSpec((None, None, rows, d), lambda i, j: (i, j, 0, 0)),
        out_shape=jax.ShapeDtypeStruct((depth, n // d, rows, d), F32),
        compiler_params=_cparams(("parallel", "parallel")),
        name="adaln",
    )(c_all, w, b.reshape(depth, 1, n))


def _proj_common(x_ref, mod_ref, npre_ref, cosk, sink, wdq_ref, qn_ref, wkv_ref, kvn_ref, ckv_ref, kr_ref):
    x = _ld(x_ref)
    rows = x.shape[0]
    h = _modulate(x, npre_ref[...], _mod(mod_ref, 0, rows), _mod(mod_ref, 1, rows)).astype(BF16)
    cq = _rms(_dot(h, wdq_ref[...]), qn_ref[...]).astype(BF16)
    kv = _dot(h, wkv_ref[...])
    ckv = _rms(kv[:, :KV_LORA], kvn_ref[...])
    krp = kv[:, KV_LORA:KV_LORA + HEAD_PAD] * cosk + kv[:, KV_LORA + HEAD_PAD:] * sink
    _st(ckv_ref, ckv)
    _st(kr_ref, krp[:, :QK_ROPE])
    return cq, ckv, krp


def _proj_prompt_kernel(x_ref, mod_ref, npre_ref, cos_ref, sina_ref, sinb_ref, wdq_ref, qn_ref, wqa_ref,
                        wkv_ref, kvn_ref, wk_ref, wvt_ref, vbias_ref,
                        q_ref, k_ref, vt_ref, ckv_ref, kr_ref):
    cos = cos_ref[...]
    sin_a = sina_ref[...]
    sin_b = sinb_ref[...]
    cq, ckv, krp = _proj_common(x_ref, mod_ref, npre_ref, cos, sin_a + sin_b, wdq_ref, qn_ref, wkv_ref, kvn_ref,
                                ckv_ref, kr_ref)
    qa = _dot(cq, wqa_ref[...])
    half = QK_ROPE // 2
    for hd in range(N_HEADS):
        x = qa[:, hd * HEAD_PAD:(hd + 1) * HEAD_PAD]
        q = x * cos + pltpu.roll(x, HEAD_PAD - half, 1) * sin_a + pltpu.roll(x, half, 1) * sin_b
        q_ref[:, hd * HEAD_PAD:(hd + 1) * HEAD_PAD] = q.astype(q_ref.dtype)
    ckv_b = ckv.astype(BF16)
    kcat = jnp.concatenate([ckv_b, krp.astype(BF16)], axis=-1)
    k_ref[...] = _dot(kcat, wk_ref[...]).astype(k_ref.dtype)
    vt = (_dot_t(wvt_ref[...], ckv_b) + vbias_ref[...]).astype(vt_ref.dtype)
    tk = vt_ref.shape[-1]
    for c in range(vt_ref.shape[0]):
        vt_ref[c] = vt[:, c * tk:(c + 1) * tk]


def _proj_prompt(x2d, mod, npre, cos_t, sin_a, sin_b, wdq, qn, wqa, wkv, kvn, wk, wvt, vbias, *, seq):
    r = _Rows(x2d, seq)
    hq = N_HEADS * HEAD_PAD
    tk = min(FLASH_TILE, seq)
    kpt = r.tm // tk
    nbp = x2d.shape[0] // seq
    vt_rows = N_HEADS * VT_ROWS
    return pl.pallas_call(
        _proj_prompt_kernel,
        grid=(r.grid,),
        in_specs=[
            r.spec(D_MODEL), r.mod_spec(), _resident((1, D_MODEL)),
            r.table_spec(HEAD_PAD), r.table_spec(HEAD_PAD), r.table_spec(HEAD_PAD),
            _resident(wdq.shape), _resident(qn.shape), _resident(wqa.shape),
            _resident(wkv.shape), _resident(kvn.shape), _resident(wk.shape), _resident(wvt.shape),
            _resident(vbias.shape),
        ],
        out_specs=[r.spec(hq), r.spec(hq),
                   pl.BlockSpec((None, kpt, vt_rows, tk), lambda i: (i // r.tps, i % r.tps, 0, 0)),
                   r.spec(KV_LORA), r.spec(QK_ROPE)],
        out_shape=[r.shape(x2d, hq, BF16), r.shape(x2d, hq, BF16),
                   jax.ShapeDtypeStruct((nbp, seq // tk, vt_rows, tk), BF16),
                   r.shape(x2d, KV_LORA, F32), r.shape(x2d, QK_ROPE, F32)],
        compiler_params=_cparams(("parallel",)),
        name="mla_proj_prompt",
    )(x2d, mod, npre, cos_t, sin_a, sin_b, wdq, qn, wqa, wkv, kvn, wk, wvt, vbias)


def _proj_sample_kernel(x_ref, mod_ref, npre_ref, cosq_ref, sinq_ref, cosk_ref, sink_ref,
                        wdq_ref, qn_ref, wqn_ref, wqa_ref, wqb_ref, wkv_ref, kvn_ref, wukt_ref,
                        ql_ref, qr_ref, ckv_ref, kr_ref):
    cq, _, _ = _proj_common(x_ref, mod_ref, npre_ref, cosk_ref[...], sink_ref[...], wdq_ref, qn_ref, wkv_ref,
                            kvn_ref, ckv_ref, kr_ref)
    q_nope = _dot(cq, wqn_ref[...]).astype(BF16)
    for hd in range(N_HEADS):
        ql = _dot(q_nope[:, hd * QK_NOPE:(hd + 1) * QK_NOPE], wukt_ref[hd])
        ql_ref[:, :, hd * KV_LORA:(hd + 1) * KV_LORA] = ql.reshape(ql_ref.shape[:2] + (KV_LORA,)).astype(ql_ref.dtype)
    _st(qr_ref, _dot(cq, wqa_ref[...]) * cosq_ref[...] + _dot(cq, wqb_ref[...]) * sinq_ref[...])


def _proj_sample(x3d, mod, npre, cosq, sinq, cosk, sink, wdq, qn, wqn, wqa, wqb, wkv, kvn, wukt):
    r = _Rows(x3d)
    return pl.pallas_call(
        _proj_sample_kernel,
        grid=(r.grid,),
        in_specs=[
            r.spec(D_MODEL), r.mod_spec(), _resident((1, D_MODEL)),
            r.table_spec(N_HEADS * QK_ROPE), r.table_spec(N_HEADS * QK_ROPE),
            r.table_spec(HEAD_PAD), r.table_spec(HEAD_PAD),
            _resident(wdq.shape), _resident(qn.shape), _resident(wqn.shape), _resident(wqa.shape),
            _resident(wqb.shape), _resident(wkv.shape), _resident(kvn.shape), _resident(wukt.shape),
        ],
        out_specs=[r.spec(N_HEADS * KV_LORA), r.spec(N_HEADS * QK_ROPE), r.spec(KV_LORA), r.spec(QK_ROPE)],
        out_shape=[r.shape(x3d, N_HEADS * KV_LORA, BF16), r.shape(x3d, N_HEADS * QK_ROPE, BF16),
                   r.shape(x3d, KV_LORA, F32), r.shape(x3d, QK_ROPE, F32)],
        compiler_params=_cparams(("parallel",)),
        name="mla_proj_sample",
    )(x3d, mod, npre, cosq, sinq, cosk, sink, wdq, qn, wqn, wqa, wqb, wkv, kvn, wukt)


def _flash_kernel(q_ref, k_ref, vt_ref, o_ref, sa_ref, sb_ref, m_ref, acc_ref, *, tq, tk, heads):
    qi = pl.program_id(2)
    below = lax.broadcasted_iota(jnp.int32, (tq, tq), 0) <= lax.broadcasted_iota(jnp.int32, (tq, tq), 1)
    qs = [q_ref[:, h * HEAD_PAD:(h + 1) * HEAD_PAD] for h in range(heads)]

    def scores(j, s_ref):
        k0 = pl.multiple_of(j * tq, tq)
        for h in range(heads):
            s_ref[h] = _dot_t(k_ref[pl.ds(k0, tq), h * HEAD_PAD:(h + 1) * HEAD_PAD], qs[h])

    def consume(j, s_ref, diagonal):
        pts, ms, alphas = [], [], []
        for h in range(heads):
            st = s_ref[h]
            if diagonal:
                st = jnp.where(below, st, -jnp.inf)
            m_old = m_ref[h]
            m_new = jnp.maximum(m_old, jnp.max(st, axis=0, keepdims=True))
            alphas.append(jnp.exp2(m_old - m_new))
            pts.append(jnp.exp2(st - m_new).astype(BF16))
            ms.append(m_new)
        for h in range(heads):
            rows = slice(h * VT_ROWS, (h + 1) * VT_ROWS)
            pv = sum(_dot(vt_ref[(tq // tk) * j + c, rows, :], pts[h][c * tk:(c + 1) * tk])
                     for c in range(tq // tk))
            acc_ref[h] = alphas[h] * acc_ref[h] + pv
            m_ref[h] = ms[h]

    m_ref[...] = jnp.full(m_ref.shape, -jnp.inf, F32)
    acc_ref[...] = jnp.zeros(acc_ref.shape, F32)
    scores(0, sa_ref)

    def two_spans(t, c):
        scores(2 * t + 1, sb_ref)
        consume(2 * t, sa_ref, False)
        scores(2 * t + 2, sa_ref)
        consume(2 * t + 1, sb_ref, False)
        return c

    lax.fori_loop(0, qi // 2, two_spans, 0)

    @pl.when(qi % 2 == 0)
    def _():
        consume(qi, sa_ref, True)

    @pl.when(qi % 2 == 1)
    def _():
        scores(qi, sb_ref)
        consume(qi - 1, sa_ref, False)
        consume(qi, sb_ref, True)

    for h in range(heads):
        acc = acc_ref[h]
        o_ref[h * V_HEAD:(h + 1) * V_HEAD, :] = (acc[:V_HEAD] / acc[V_HEAD:V_HEAD + 1]).astype(o_ref.dtype)


def _flash(q, k, vt):
    nb, seq, _ = q.shape
    tk = vt.shape[-1]
    tq = 2 * tk
    heads = FLASH_HEADS
    assert seq % tq == 0 and N_HEADS % heads == 0
    return pl.pallas_call(
        functools.partial(_flash_kernel, tq=tq, tk=tk, heads=heads),
        grid=(nb, N_HEADS // heads, seq // tq),
        in_specs=[
            pl.BlockSpec((None, tq, heads * HEAD_PAD), lambda b, g, i: (b, i, g)),
            pl.BlockSpec((None, seq, heads * HEAD_PAD), lambda b, g, i: (b, 0, g)),
            pl.BlockSpec((None, seq // tk, heads * VT_ROWS, tk), lambda b, g, i: (b, 0, g, 0)),
        ],
        out_specs=pl.BlockSpec((None, heads * V_HEAD, tq), lambda b, g, i: (b, g, i)),
        out_shape=jax.ShapeDtypeStruct((nb, N_HEADS * V_HEAD, seq), BF16),
        scratch_shapes=[pltpu.VMEM((heads, tq, tq), F32), pltpu.VMEM((heads, tq, tq), F32),
                        pltpu.VMEM((heads, 1, tq), F32), pltpu.VMEM((heads, VT_ROWS, tq), F32)],
        compiler_params=_cparams(("parallel", "parallel", "arbitrary")),
        name="flash_prompt",
    )(q, k, vt)


def _sattn_kernel(pt_ref, ql_ref, qr_ref, cn_ref, rn_ref, ckv_hbm, krt_hbm, o_ref, kvbuf, krbuf, sem,
                  s_a, s_b, kv_a, kv_b, *, layer, n_pages, chunk_pages):
    b = pl.program_id(0)
    nb = pl.num_programs(0)
    slot = b % 2

    def page_copies(bb, sl, p):
        page = pt_ref[bb * n_pages + p]
        return (pltpu.make_async_copy(ckv_hbm.at[layer, page], kvbuf.at[sl, p], sem.at[0, sl]),
                pltpu.make_async_copy(krt_hbm.at[layer, page], krbuf.at[sl, p], sem.at[1, sl]))

    def start_fetch(bb, sl):
        for p in range(n_pages):
            for cp in page_copies(bb, sl, p):
                cp.start()

    def wait_fetch(bb, sl):
        for p in range(n_pages):
            for cp in page_copies(bb, sl, p):
                cp.wait()

    @pl.when(b == 0)
    def _():
        start_fetch(0, 0)

    @pl.when(b + 1 < nb)
    def _():
        start_fetch(b + 1, 1 - slot)

    wait_fetch(b, slot)

    ql = ql_ref[...].reshape(-1, KV_LORA)
    qr = qr_ref[...].reshape(-1, QK_ROPE)
    rows = ql.shape[0]
    ck = chunk_pages * PAGE_SIZE

    s_bufs = (s_a, s_b)
    kv_bufs = (kv_a, kv_b)

    def scores(c):
        p0 = c * chunk_pages
        kv = kvbuf[slot, p0:p0 + chunk_pages].reshape(ck, KV_LORA).astype(BF16)
        kv_bufs[c % 2][...] = kv
        krt = jnp.concatenate([krbuf[slot, p0 + i] for i in range(chunk_pages)], axis=1).astype(BF16)
        s_bufs[c % 2][...] = _dot_t(ql, kv) + _dot(qr, krt)

    def softmax_step(s, values, carry):
        m, l, acc = carry
        m_new = jnp.maximum(m, jnp.max(s, axis=-1, keepdims=True))
        alpha = jnp.exp(m - m_new)
        p = jnp.exp(s - m_new)
        l = alpha * l + jnp.sum(p, axis=-1, keepdims=True)
        return m_new, l, alpha * acc + _dot(p.astype(BF16), values)

    carry = (jnp.full((rows, 1), -jnp.inf, F32), jnp.zeros((rows, 1), F32), jnp.zeros((rows, KV_LORA), F32))
    n_chunks = n_pages // chunk_pages
    scores(0)
    for c in range(n_chunks):
        if c + 1 < n_chunks:
            scores(c + 1)
        carry = softmax_step(s_bufs[c % 2][...], kv_bufs[c % 2][...], carry)

    cn = cn_ref[...].astype(BF16)
    rn = rn_ref[...].astype(BF16)
    s = _dot_t(ql, cn) + _dot_t(qr, rn)
    qstep = lax.broadcasted_iota(jnp.int32, s.shape, 0) // N_HEADS
    kstep = lax.broadcasted_iota(jnp.int32, s.shape, 1)
    s = jnp.where(kstep <= qstep, s, -jnp.inf)
    _, l, acc = softmax_step(s, cn, carry)
    o_ref[...] = (acc / l).astype(o_ref.dtype).reshape(o_ref.shape)


def _sattn(page_table, ql, qr, cn, rn, cache_kv, cache_krt, *, layer, chunk_pages):
    steps, nb = ql.shape[0], ql.shape[1]
    rows = steps * N_HEADS
    n_pages = page_table.shape[1]
    n_new = cn.shape[1]
    ck = chunk_pages * PAGE_SIZE
    qspec = lambda w: pl.BlockSpec((steps, None, N_HEADS, w), lambda b, pt: (0, b, 0, 0))
    grid_spec = pltpu.PrefetchScalarGridSpec(
        num_scalar_prefetch=1,
        grid=(nb,),
        in_specs=[
            qspec(KV_LORA),
            qspec(QK_ROPE),
            pl.BlockSpec((None, n_new, KV_LORA), lambda b, pt: (b, 0, 0)),
            pl.BlockSpec((None, n_new, QK_ROPE), lambda b, pt: (b, 0, 0)),
            pl.BlockSpec(memory_space=pl.ANY),
            pl.BlockSpec(memory_space=pl.ANY),
        ],
        out_specs=qspec(KV_LORA),
        scratch_shapes=[
            pltpu.VMEM((2, n_pages, PAGE_SIZE, KV_LORA), F32),
            pltpu.VMEM((2, n_pages, QK_ROPE, PAGE_SIZE), F32),
            pltpu.SemaphoreType.DMA((2, 2)),
            pltpu.VMEM((rows, ck), F32), pltpu.VMEM((rows, ck), F32),
            pltpu.VMEM((ck, KV_LORA), BF16), pltpu.VMEM((ck, KV_LORA), BF16),
        ],
    )
    return pl.pallas_call(
        functools.partial(_sattn_kernel, layer=layer, n_pages=n_pages, chunk_pages=chunk_pages),
        grid_spec=grid_spec,
        out_shape=jax.ShapeDtypeStruct((steps, nb, N_HEADS, KV_LORA), BF16),
        compiler_params=_cparams(("arbitrary",)),
        name="sample_attend",
    )(page_table.reshape(-1), ql, qr, cn, rn, cache_kv, cache_krt)


def _mla_out_kernel(*refs, absorbed):
    if absorbed:
        o_ref, x_ref, mod_ref, npost_ref, wuv_ref, wo_ref, y_ref = refs
        o_lat = _ld(o_ref)
        o = jnp.concatenate([_dot(o_lat[:, hd * KV_LORA:(hd + 1) * KV_LORA], wuv_ref[hd])
                             for hd in range(N_HEADS)], axis=-1).astype(BF16)
        f = _dot(o, wo_ref[...])
    else:
        o_ref, x_ref, mod_ref, npost_ref, wo_ref, y_ref = refs
        f = lax.dot_general(o_ref[...], wo_ref[...], (((0,), (0,)), ((), ())), preferred_element_type=F32)
    x = _ld(x_ref)
    _st(y_ref, x + _mod(mod_ref, 2, x.shape[0]) * _rms(f, npost_ref[...]))


def _mla_out(o, x, mod, npost, wo, wuv=None, *, seq=None):
    r = _Rows(x, seq)
    if wuv is not None:
        o_spec = r.spec(o.shape[-1])
    else:
        o_spec = pl.BlockSpec((None, o.shape[1], r.tm), lambda i: (i // r.tps, 0, i % r.tps))
    in_specs = [o_spec, r.spec(D_MODEL), r.mod_spec(), _resident((1, D_MODEL))]
    args = [o, x, mod, npost]
    if wuv is not None:
        in_specs.append(_resident(wuv.shape))
        args.append(wuv)
    in_specs.append(_resident(wo.shape))
    args.append(wo)
    return pl.pallas_call(
        functools.partial(_mla_out_kernel, absorbed=wuv is not None),
        grid=(r.grid,),
        in_specs=in_specs,
        out_specs=r.spec(D_MODEL),
        out_shape=r.shape(x, D_MODEL, F32),
        compiler_params=_cparams(("parallel",)),
        name="mla_out_sample" if wuv is not None else "mla_out_prompt",
    )(*args)


def _pool_finish(window_means, h, x, mod_ref, npost_ref, wg_ref, ls_ref, y_ref):
    outs = []
    for g in range(len(POOL_WINDOWS)):
        cols = slice(g * GROUP_W, (g + 1) * GROUP_W)
        pooled = (window_means[g] - h[:, cols]).astype(BF16)
        outs.append(_dot(pooled, wg_ref[g]))
    o = jnp.concatenate(outs, axis=-1) * ls_ref[...]
    _st(y_ref, x + _mod(mod_ref, 2, x.shape[0]) * _rms(o, npost_ref[...]))


def _pool_prompt_kernel(x_ref, xh_ref, mod_ref, npre_ref, npost_ref, wg_ref, ls_ref, y_ref, st_ref, hext_ref,
                        *, tm, tps):
    t_in_seq = pl.program_id(0) % tps
    keep = jnp.where(t_in_seq == 0, 0.0, 1.0)
    x = x_ref[...]
    h = _modulate(x, npre_ref[...], mod_ref[0], mod_ref[1])
    hh = _modulate(xh_ref[...], npre_ref[...], mod_ref[0], mod_ref[1]) * keep
    hext_ref[0:POOL_HALO, :] = hh
    hext_ref[POOL_HALO:, :] = h
    st_ref[...] = h[tm - POOL_HALO:, :]
    pos = t_in_seq * tm + lax.broadcasted_iota(jnp.int32, (tm, 1), 0)
    means = []
    for g, w in enumerate(POOL_WINDOWS):
        cols = slice(g * GROUP_W, (g + 1) * GROUP_W)
        s = hext_ref[:, cols]
        shift = 1
        while shift < w:
            s = s + pltpu.roll(s, shift, 0)
            shift *= 2
        means.append(s[POOL_HALO:] / jnp.minimum(pos + 1, w).astype(F32))
    _pool_finish(means, h, x, mod_ref, npost_ref, wg_ref, ls_ref, y_ref)


def _pool_prompt(x2d, mod, npre, npost, wg, ls, *, seq):
    r = _Rows(x2d, seq)
    hb = r.tm // POOL_HALO
    return pl.pallas_call(
        functools.partial(_pool_prompt_kernel, tm=r.tm, tps=r.tps),
        grid=(r.grid,),
        in_specs=[
            r.spec(D_MODEL),
            pl.BlockSpec((POOL_HALO, D_MODEL), lambda i: (jnp.maximum(i * hb - 1, 0), 0)),
            r.mod_spec(),
            _resident((1, D_MODEL)), _resident((1, D_MODEL)), _resident(wg.shape), _resident((1, D_MODEL)),
        ],
        out_specs=[r.spec(D_MODEL), pl.BlockSpec((None, POOL_HALO, D_MODEL), lambda i: (i // r.tps, 0, 0))],
        out_shape=[r.shape(x2d, D_MODEL, F32),
                   jax.ShapeDtypeStruct((x2d.shape[0] // seq, POOL_HALO, D_MODEL), F32)],
        scratch_shapes=[pltpu.VMEM((r.tm + POOL_HALO, D_MODEL), F32)],
        compiler_params=_cparams(("arbitrary",)),
        name="pool_prompt",
    )(x2d, x2d, mod, npre, npost, wg, ls)


def _pool_sample_kernel(x_ref, st_ref, mod_ref, npre_ref, npost_ref, wg_ref, ls_ref, y_ref, so_ref, *, past):
    steps, nbt, _ = x_ref.shape
    x = _ld(x_ref)
    h = _modulate(x, npre_ref[...], _mod(mod_ref, 0, x.shape[0]), _mod(mod_ref, 1, x.shape[0]))

    def slab(k, cols):
        if k < POOL_BUF:
            return st_ref[k, :, cols]
        return h[(k - POOL_BUF) * nbt:(k - POOL_BUF + 1) * nbt, cols]

    for k in range(POOL_BUF):
        so_ref[k] = slab(k + steps, slice(None))
    means = []
    for g, w in enumerate(POOL_WINDOWS):
        cols = slice(g * GROUP_W, (g + 1) * GROUP_W)
        parts = []
        for s_ in range(steps):
            acc = slab(POOL_BUF + s_, cols)
            for j in range(1, w):
                acc = acc + slab(POOL_BUF + s_ - j, cols)
            parts.append(acc / float(min(past + s_ + 1, w)))
        means.append(jnp.concatenate(parts, axis=0))
    _pool_finish(means, h, x, mod_ref, npost_ref, wg_ref, ls_ref, y_ref)


def _pool_sample(x3d, st3d, mod, npre, npost, wg, ls, *, past):
    r = _Rows(x3d)
    st_spec = pl.BlockSpec((POOL_BUF, r.nbt, D_MODEL), lambda i: (0, i, 0))
    return pl.pallas_call(
        functools.partial(_pool_sample_kernel, past=past),
        grid=(r.grid,),
        in_specs=[r.spec(D_MODEL), st_spec, r.mod_spec(),
                  _resident((1, D_MODEL)), _resident((1, D_MODEL)), _resident(wg.shape), _resident((1, D_MODEL))],
        out_specs=[r.spec(D_MODEL), st_spec],
        out_shape=[r.shape(x3d, D_MODEL, F32), jax.ShapeDtypeStruct(st3d.shape, F32)],
        compiler_params=_cparams(("parallel",)),
        name="pool_sample",
    )(x3d, st3d, mod, npre, npost, wg, ls)


def _ffn_chunks(h, bufs, put_fn, conv_fn, cw_ref, cb_ref, wup_ref, wdn_ref):
    n = D_FF // FF_CHUNK
    cols = lambda c, off: slice(off + c * FF_CHUNK, off + (c + 1) * FF_CHUNK)

    def up(c):
        for k, off in enumerate((0, D_FF)):
            put_fn(bufs[c % 2], k, _dot(h, wup_ref[:, cols(c, off)]))

    up(0)
    acc = None
    for c in range(n):
        halves = []
        for k, off in enumerate((0, D_FF)):
            u, r1, r2 = conv_fn(bufs[c % 2], k, cols(c, off))
            cw = cw_ref[:, cols(c, off)]
            halves.append(cb_ref[:, cols(c, off)] + r2 * cw[0:1] + r1 * cw[1:2] + u * cw[2:3])
        if c + 1 < n:
            up(c + 1)
        gate = (_silu(halves[0]) * halves[1]).astype(BF16)
        part = _dot(gate, wdn_ref[c * FF_CHUNK:(c + 1) * FF_CHUNK, :])
        acc = part if acc is None else acc + part
    return acc


def _ffn_prompt_kernel(x_ref, xh_ref, mod_ref, npre_ref, npost_ref, wup_ref, cw_ref, cb_ref, wdn_ref,
                       y_ref, st_ref, buf_a, buf_b, *, tm, tps):
    keep = jnp.where(pl.program_id(0) % tps == 0, 0.0, 1.0)
    x = x_ref[...]
    xe = jnp.concatenate([xh_ref[...], x], axis=0)
    he = _modulate(xe, npre_ref[...], mod_ref[0], mod_ref[1]).astype(BF16)
    hl = CONV_HALO

    def put_fn(buf, k, ue):
        buf[k, :hl] = ue[:hl] * keep
        buf[k, hl:] = ue[hl:]

    def conv_fn(buf, k, cols):
        st_ref[:, cols] = buf[k, hl + tm - STATE_ROWS:hl + tm]
        return buf[k, hl:hl + tm], buf[k, hl - 1:hl - 1 + tm], buf[k, hl - 2:hl - 2 + tm]

    f = _ffn_chunks(he, (buf_a, buf_b), put_fn, conv_fn, cw_ref, cb_ref, wup_ref, wdn_ref)
    y_ref[...] = x + mod_ref[2] * _rms(f, npost_ref[...])


def _ffn_prompt(x2d, mod, npre, npost, wup, cw, cb, wdn, *, layer, seq):
    r = _Rows(x2d, seq, FFN_TILE)
    hb = r.tm // CONV_HALO
    lyr = lambda a: pl.BlockSpec((None,) + a.shape[1:], lambda i: (layer,) + (0,) * (a.ndim - 1),
                                 pipeline_mode=pl.Buffered(1))
    return pl.pallas_call(
        functools.partial(_ffn_prompt_kernel, tm=r.tm, tps=r.tps),
        grid=(r.grid,),
        in_specs=[
            r.spec(D_MODEL),
            pl.BlockSpec((CONV_HALO, D_MODEL), lambda i: (jnp.maximum(i * hb - 1, 0), 0)),
            r.mod_spec(),
            _resident((1, D_MODEL)), _resident((1, D_MODEL)),
            lyr(wup), _resident(cw.shape), _resident(cb.shape), lyr(wdn),
        ],
        out_specs=[r.spec(D_MODEL), pl.BlockSpec((None, STATE_ROWS, 2 * D_FF), lambda i: (i // r.tps, 0, 0))],
        out_shape=[r.shape(x2d, D_MODEL, F32),
                   jax.ShapeDtypeStruct((x2d.shape[0] // seq, STATE_ROWS, 2 * D_FF), F32)],
        scratch_shapes=[pltpu.VMEM((2, r.tm + CONV_HALO, FF_CHUNK), F32)] * 2,
        compiler_params=_cparams(("arbitrary",)),
        name="ffn_prompt",
    )(x2d, x2d, mod, npre, npost, wup, cw, cb, wdn)


def _ffn_sample_kernel(x_ref, st_ref, mod_ref, npre_ref, npost_ref, wup_ref, cw_ref, cb_ref, wdn_ref,
                       y_ref, so_ref, buf_a, buf_b):
    steps, nbt, _ = x_ref.shape
    x = _ld(x_ref)
    rows = x.shape[0]
    h = _modulate(x, npre_ref[...], _mod(mod_ref, 0, rows), _mod(mod_ref, 1, rows)).astype(BF16)

    def put_fn(buf, k, u):
        buf[k] = u

    def conv_fn(buf, k, cols):
        s0 = st_ref[0, :, cols]
        s1 = st_ref[1, :, cols]
        so_ref[0, :, cols] = buf[k, rows - 2 * nbt:rows - nbt]
        so_ref[1, :, cols] = buf[k, rows - nbt:rows]
        return (buf[k], jnp.concatenate([s1, buf[k, :rows - nbt]], axis=0),
                jnp.concatenate([s0, s1, buf[k, :rows - 2 * nbt]], axis=0))

    f = _ffn_chunks(h, (buf_a, buf_b), put_fn, conv_fn, cw_ref, cb_ref, wup_ref, wdn_ref)
    _st(y_ref, x + _mod(mod_ref, 2, rows) * _rms(f, npost_ref[...]))


def _ffn_sample(x3d, st3d, mod, npre, npost, wup, cw, cb, wdn, *, layer):
    r = _Rows(x3d)
    st_spec = pl.BlockSpec((CONV_BUF, r.nbt, 2 * D_FF), lambda i: (0, i, 0))
    lyr = lambda a: pl.BlockSpec((None,) + a.shape[1:], lambda i: (layer,) + (0,) * (a.ndim - 1),
                                 pipeline_mode=pl.Buffered(1))
    return pl.pallas_call(
        _ffn_sample_kernel,
        grid=(r.grid,),
        in_specs=[r.spec(D_MODEL), st_spec, r.mod_spec(),
                  _resident((1, D_MODEL)), _resident((1, D_MODEL)),
                  lyr(wup), _resident(cw.shape), _resident(cb.shape), lyr(wdn)],
        out_specs=[r.spec(D_MODEL), st_spec],
        out_shape=[r.shape(x3d, D_MODEL, F32), jax.ShapeDtypeStruct(st3d.shape, F32)],
        scratch_shapes=[pltpu.VMEM((2, r.tile_rows, FF_CHUNK), F32)] * 2,
        compiler_params=_cparams(("parallel",)),
        name="ffn_sample",
    )(x3d, st3d, mod, npre, npost, wup, cw, cb, wdn)


def _rope_angles(pos):
    half = QK_ROPE // 2
    inv = ROPE_THETA ** (-jnp.arange(half, dtype=F32) / half)
    ang = pos.astype(F32)[:, None] * inv[None, :]
    return jnp.cos(ang), jnp.sin(ang)


def _rope_tables(pos, width):
    cos, sin = _rope_angles(pos)
    n = pos.shape[0]
    cos_t = jnp.concatenate([cos, cos, jnp.ones((n, width - QK_ROPE), F32)], axis=-1)
    sin_t = jnp.concatenate([-sin, sin, jnp.zeros((n, width - QK_ROPE), F32)], axis=-1)
    return cos_t, sin_t


def _swap_halves(w):
    half = QK_ROPE // 2
    return jnp.concatenate([w[..., half:], w[..., :half]], axis=-1)


def _mla_weights(w_dq, w_uq, w_dkv, w_uk, w_uv, w_o):
    wq3 = w_uq.reshape(Q_LORA, N_HEADS, QK_NOPE + QK_ROPE) * ATTN_SCALE
    nope, ropew = wq3[..., :QK_NOPE], wq3[..., QK_NOPE:]
    ropesw = _swap_halves(ropew)
    zpad = jnp.zeros((Q_LORA, N_HEADS, HEAD_PAD - QK_NOPE - QK_ROPE), F32)
    wqa_p = jnp.concatenate([ropew, nope, zpad], axis=-1).reshape(Q_LORA, N_HEADS * HEAD_PAD)
    kr_w = w_dkv[:, KV_LORA:]
    lane_pad = jnp.zeros((D_MODEL, HEAD_PAD - QK_ROPE), F32)
    wkv = jnp.concatenate([w_dkv[:, :KV_LORA], kr_w, lane_pad, _swap_halves(kr_w), lane_pad], axis=-1)
    eye_r = jnp.eye(QK_ROPE, dtype=F32)
    top = jnp.concatenate([jnp.zeros((KV_LORA, N_HEADS, QK_ROPE), F32), w_uk,
                           jnp.zeros((KV_LORA, N_HEADS, HEAD_PAD - QK_NOPE - QK_ROPE), F32)], axis=-1)
    mid = jnp.concatenate([jnp.broadcast_to(eye_r[:, None, :], (QK_ROPE, N_HEADS, QK_ROPE)),
                           jnp.zeros((QK_ROPE, N_HEADS, HEAD_PAD - QK_ROPE), F32)], axis=-1)
    bot = jnp.zeros((HEAD_PAD - QK_ROPE, N_HEADS, HEAD_PAD), F32)
    wk = jnp.concatenate([top, mid, bot], axis=0).reshape(KV_LORA + HEAD_PAD, N_HEADS * HEAD_PAD)
    wvt = jnp.pad(jnp.transpose(w_uv, (1, 2, 0)), ((0, 0), (0, VT_ROWS - V_HEAD), (0, 0)))
    wvt = wvt.reshape(N_HEADS * VT_ROWS, KV_LORA)
    vbias = jnp.tile((jnp.arange(VT_ROWS) == V_HEAD).astype(F32), N_HEADS).reshape(N_HEADS * VT_ROWS, 1)
    wukt = jnp.transpose(w_uk, (1, 2, 0))
    wuv = jnp.transpose(w_uv, (1, 0, 2))
    b = lambda a: a.astype(BF16)
    return dict(
        wdq=b(w_dq), wqa_p=b(wqa_p * LOG2E), wkv=b(wkv), wk=b(wk), wvt=b(wvt), vbias=vbias,
        wqn=b(nope.reshape(Q_LORA, N_HEADS * QK_NOPE)),
        wqa_s=b(ropew.reshape(Q_LORA, N_HEADS * QK_ROPE)), wqb_s=b(ropesw.reshape(Q_LORA, N_HEADS * QK_ROPE)),
        wukt=b(wukt), wuv=b(wuv), wo=b(w_o))


def kernel(x_prompt, x_sample, cache_kv_latent, cache_k_rope, state_pool, state_conv, page_table, c_prompt, c_sample, ada_w, ada_b, norm_mix_pre, norm_mix_post, norm_ffn_pre, norm_ffn_post, mla_w_dq, mla_q_norm, mla_w_uq, mla_w_dkv, mla_kv_norm, mla_w_uk, mla_w_uv, mla_w_o, pool_w, pool_scale, ffn_w_up, ffn_conv_w, ffn_conv_b, ffn_w_down):
    nbp, seq, d = x_prompt.shape
    nbs, steps, _ = x_sample.shape
    depth = ada_w.shape[0]
    past = page_table.shape[1] * PAGE_SIZE
    row = lambda a: a.reshape(1, -1)
    tmajor = lambda a: jnp.transpose(a, (1, 0, 2))

    c_all = jnp.concatenate([c_prompt, c_sample], axis=0)
    c_all = jnp.pad(c_all, ((0, (-c_all.shape[0]) % 8), (0, 0)))
    mods = _adaln(c_all, ada_w, ada_b)

    def mods_prompt(i, k0):
        return mods[i, k0:k0 + 3, :nbp, None, :]

    def mods_sample(i, k0):
        return mods[i, k0:k0 + 3, nbp:nbp + nbs]

    wup_all = ffn_w_up.astype(BF16)
    wdn_all = ffn_w_down.astype(BF16)
    xp = x_prompt.reshape(nbp * seq, d)
    xs = tmajor(x_sample)
    nbt = min(SAMPLE_TILE, nbs)

    outs = dict(kv_p=[], kr_p=[], kv_s=[], kr_s=[], pool_p=[], pool_s=[], conv_p=[], conv_s=[])
    for i in range(depth):
        j = i // 2
        if i % 2 == 0:
            w = _mla_weights(mla_w_dq[j], mla_w_uq[j], mla_w_dkv[j], mla_w_uk[j], mla_w_uv[j], mla_w_o[j])
            cos_p, sin_p = _rope_tables(jnp.arange(seq), HEAD_PAD)
            first_half = jnp.arange(HEAD_PAD) < QK_ROPE // 2
            sin_a, sin_b = jnp.where(first_half, sin_p, 0.0), jnp.where(first_half, 0.0, sin_p)
            mp = mods_prompt(i, 0)
            q, k, vt, ckv, kr = _proj_prompt(
                xp, mp, row(norm_mix_pre[i]), cos_p, sin_a, sin_b, w['wdq'], row(mla_q_norm[j]), w['wqa_p'],
                w['wkv'], row(mla_kv_norm[j]), w['wk'], w['wvt'], w['vbias'], seq=seq)
            hq = N_HEADS * HEAD_PAD
            o_t = _flash(q.reshape(nbp, seq, hq), k.reshape(nbp, seq, hq), vt)
            xp = _mla_out(o_t, xp, mp, row(norm_mix_post[i]), w['wo'], seq=seq)
            outs['kv_p'].append(ckv.reshape(nbp, seq, KV_LORA))
            outs['kr_p'].append(kr.reshape(nbp, seq, QK_ROPE))
            cos_k, sin_k = _rope_tables(past + jnp.repeat(jnp.arange(steps), nbt), HEAD_PAD)
            cos_q = jnp.tile(cos_k[:, :QK_ROPE], (1, N_HEADS))
            sin_q = jnp.tile(sin_k[:, :QK_ROPE], (1, N_HEADS))
            ms = mods_sample(i, 0)
            ql, qr, ckv_s, kr_s = _proj_sample(
                xs, ms, row(norm_mix_pre[i]), cos_q, sin_q, cos_k, sin_k, w['wdq'], row(mla_q_norm[j]),
                w['wqn'], w['wqa_s'], w['wqb_s'], w['wkv'], row(mla_kv_norm[j]), w['wukt'])
            ckv_sb = tmajor(ckv_s)
            kr_sb = tmajor(kr_s)
            key_pad = ((0, 0), (0, NEW_KEY_ROWS - steps), (0, 0))
            o_lat = _sattn(page_table, ql.reshape(steps, nbs, N_HEADS, KV_LORA),
                           qr.reshape(steps, nbs, N_HEADS, QK_ROPE),
                           jnp.pad(ckv_sb, key_pad), jnp.pad(kr_sb, key_pad),
                           cache_kv_latent, jnp.swapaxes(cache_k_rope, 2, 3), layer=j, chunk_pages=16)
            xs = _mla_out(o_lat.reshape(steps, nbs, N_HEADS * KV_LORA), xs, ms, row(norm_mix_post[i]),
                          w['wo'], w['wuv'])
            outs['kv_s'].append(ckv_sb)
            outs['kr_s'].append(kr_sb)
        else:
            wg = pool_w[j].astype(BF16)
            xp, st_p = _pool_prompt(xp, mods_prompt(i, 0), row(norm_mix_pre[i]), row(norm_mix_post[i]), wg,
                                    row(pool_scale[j]), seq=seq)
            outs['pool_p'].append(st_p[:, POOL_HALO - POOL_BUF:])
            xs, st_s = _pool_sample(xs, tmajor(state_pool[j]), mods_sample(i, 0), row(norm_mix_pre[i]),
                                    row(norm_mix_post[i]), wg, row(pool_scale[j]), past=past)
            outs['pool_s'].append(tmajor(st_s))

        xp, cst_p = _ffn_prompt(xp, mods_prompt(i, 3), row(norm_ffn_pre[i]), row(norm_ffn_post[i]), wup_all,
                                ffn_conv_w[i], row(ffn_conv_b[i]), wdn_all, layer=i, seq=seq)
        outs['conv_p'].append(cst_p[:, STATE_ROWS - CONV_BUF:])
        xs, cst_s = _ffn_sample(xs, tmajor(state_conv[i]), mods_sample(i, 3), row(norm_ffn_pre[i]),
                                row(norm_ffn_post[i]), wup_all, ffn_conv_w[i], row(ffn_conv_b[i]), wdn_all,
                                layer=i)
        outs['conv_s'].append(tmajor(cst_s))

    st = lambda key: jnp.stack(outs[key])
    return (xp.reshape(nbp, seq, d), tmajor(xs), st('kv_p'), st('kr_p'), st('kv_s'), st('kr_s'),
            st('pool_p'), st('pool_s'), st('conv_p'), st('conv_s'))
```
